```python
import math
import jax
import jax.numpy as jnp
from jax import lax
import numpy as np

D_MODEL = 1024
BATCH = 2
SEQ = 16384
DEPTH = 2
DEC_BATCH = 8
DEC_SEQ = 32
PAST_LEN = 1024

CHUNK = 64
Q_BLOCK = 128
ROPE_THETA = 10000.0
EPS = 1e-6
N_EVEN = (DEPTH + 1) // 2
N_ODD = DEPTH // 2
W_M = D_MODEL // 2
DH_M = 128
H_M = W_M // DH_M
W_R = D_MODEL // 2
DH_R = 128
H_R = W_R // DH_R
W_P = D_MODEL // 2
POOL_WINDOWS = (2, 4, 8, 16)
N_POOL_GROUPS = len(POOL_WINDOWS)
GP = W_P // N_POOL_GROUPS
POOL_MAX = max(POOL_WINDOWS)
W_D = D_MODEL // 2
D_DIFF = 64
H_D = W_D // (2 * D_DIFF)
N_EXPERTS = 16
N_GROUPS = 4
EPG = N_EXPERTS // N_GROUPS
TOP_K = 2
D_FF_EXPERT = D_MODEL // 2
IN_EVEN = 4 * W_M + 2 * H_M + 4 * W_R
IN_ODD = W_P + 3 * W_D
EVEN_SPLITS = (W_M, 2 * W_M, 3 * W_M, 4 * W_M, 4 * W_M + H_M, 4 * W_M + 2 * H_M, 4 * W_M + 2 * H_M + W_R, 4 * W_M + 2 * H_M + 2 * W_R, 4 * W_M + 2 * H_M + 3 * W_R)
ODD_SPLITS = (W_P, W_P + W_D, W_P + 2 * W_D)

kernel_name = 'streaming_mlstm_retention_pool_diffattn_moe'


def rms_norm(x, g):
    xf = x.astype(jnp.float32)
    y = xf * lax.rsqrt(jnp.mean(xf * xf, axis=-1, keepdims=True) + EPS)
    return (y * g.astype(jnp.float32)).astype(x.dtype)


def rope(x, pos):
    d = x.shape[-1]
    half = d // 2
    inv = jnp.power(ROPE_THETA, -jnp.arange(half, dtype=jnp.float32) * 2.0 / d)
    ang = pos.astype(jnp.float32)[:, None] * inv[None, :]
    shape = (ang.shape[0],) + (1,) * (x.ndim - 3) + (half,)
    cos = jnp.cos(ang).reshape(shape)
    sin = jnp.sin(ang).reshape(shape)
    x1, x2 = x[..., :half], x[..., half:]
    return jnp.concatenate([x1 * cos - x2 * sin, x1 * sin + x2 * cos], axis=-1).astype(x.dtype)


def mlstm_chunkwise(q, k, v, ig, lf, C0, n0, m0):
    B, T, H, d = q.shape
    L = min(CHUNK, T)
    N = T // L
    blk = lambda a: a.reshape(B, N, L, H, -1).transpose(0, 3, 1, 2, 4)
    q, k, v = blk(q), blk(k) * (d ** -0.5), blk(v)
    ig = ig.reshape(B, N, L, H).transpose(0, 3, 1, 2)
    lf = lf.reshape(B, N, L, H).transpose(0, 3, 1, 2)
    b = jnp.cumsum(lf, axis=-1)
    tril = jnp.tril(jnp.ones((L, L), dtype=bool))
    Dm = jnp.where(tril, b[..., :, None] - b[..., None, :] + ig[..., None, :], -jnp.inf)
    g_end = b[..., -1:] - b + ig
    a_max = jnp.max(g_end, axis=-1)
    w_end = jnp.exp(g_end - a_max[..., None])
    kv_c = jnp.einsum('bhnl,bhnld,bhnle->bhnde', w_end, k, v)
    ks_c = jnp.einsum('bhnl,bhnld->bhnd', w_end, k)
    b_end = b[..., -1]

    def step(carry, inp):
        C, n, m = carry
        kv, ks, be, am = inp
        m_new = jnp.maximum(be + m, am)
        f_s = jnp.exp(be + m - m_new)
        i_s = jnp.exp(am - m_new)
        C_new = f_s[..., None, None] * C + i_s[..., None, None] * kv
        n_new = f_s[..., None] * n + i_s[..., None] * ks
        return (C_new, n_new, m_new), (C, n, m)

    xs = (jnp.moveaxis(kv_c, 2, 0), jnp.moveaxis(ks_c, 2, 0), jnp.moveaxis(b_end, 2, 0), jnp.moveaxis(a_max, 2, 0))
    (Cf, nf, mf), (Cs, ns, ms) = lax.scan(step, (C0, n0, m0), xs)
    Cs = jnp.moveaxis(Cs, 0, 2)
    ns = jnp.moveaxis(ns, 0, 2)
    ms = jnp.moveaxis(ms, 0, 2)
    m_inter = b + ms[..., None]
    m_t = jnp.maximum(m_inter, jnp.max(Dm, axis=-1))
    P = jnp.exp(Dm - m_t[..., None]) * jnp.einsum('bhnld,bhnsd->bhnls', q, k)
    sc = jnp.exp(m_inter - m_t)
    num = sc[..., None] * jnp.einsum('bhnld,bhnde->bhnle', q, Cs) + jnp.einsum('bhnls,bhnse->bhnle', P, v)
    den = sc * jnp.einsum('bhnld,bhnd->bhnl', q, ns) + jnp.sum(P, axis=-1)
    h = num / jnp.maximum(jnp.abs(den), jnp.exp(-m_t))[..., None]
    h = h.transpose(0, 2, 3, 1, 4).reshape(B, T, H, d)
    return h, Cf, nf, mf


def retention_chunkwise(q, k, v, S0):
    B, T, H, d = q.shape
    L = min(CHUNK, T)
    N = T // L
    blk = lambda a: a.reshape(B, N, L, H, d).transpose(0, 3, 1, 2, 4)
    q, k, v = blk(q), blk(k) * (d ** -0.5), blk(v)
    lg = jnp.log(1.0 - jnp.power(2.0, -5.0 - jnp.arange(H, dtype=jnp.float32)))
    j = jnp.arange(L, dtype=jnp.float32)
    rel = j[:, None] - j[None, :]
    decay = jnp.where(rel >= 0, jnp.exp(lg[:, None, None] * jnp.maximum(rel, 0.0)), 0.0)
    scores = jnp.einsum('bhnld,bhnsd->bhnls', q, k) * decay[None, :, None]
    intra = jnp.einsum('bhnls,bhnse->bhnle', scores, v)
    w_end = jnp.exp(lg[:, None] * (L - 1 - j)[None, :])
    kv_c = jnp.einsum('hl,bhnld,bhnle->bhnde', w_end, k, v)
    g_chunk = jnp.exp(lg * L)[:, None, None]

    def step(S, kv):
        return g_chunk * S + kv, S

    S_f, S_starts = lax.scan(step, S0, jnp.moveaxis(kv_c, 2, 0))
    S_starts = jnp.moveaxis(S_starts, 0, 2)
    inter = jnp.einsum('bhnld,bhnde->bhnle', q, S_starts) * jnp.exp(lg[:, None] * (j + 1.0)[None, :])[None, :, None, :, None]
    o = (intra + inter).transpose(0, 2, 3, 1, 4).reshape(B, T, H, d)
    return o, S_f


def multi_scale_pool(u, past, pos):
    B, T, W = u.shape
    P = POOL_MAX - 1
    full = jnp.concatenate([past.astype(u.dtype), u], axis=1)
    cs = jnp.cumsum(full.astype(jnp.float32), axis=1)
    cs = jnp.concatenate([jnp.zeros((B, 1, W), jnp.float32), cs], axis=1)
    means = []
    for g, w in enumerate(POOL_WINDOWS):
        sl = slice(g * GP, (g + 1) * GP)
        win = cs[:, P + 1:P + 1 + T, sl] - cs[:, P + 1 - w:P + 1 - w + T, sl]
        cnt = jnp.minimum(pos + 1, w).astype(jnp.float32)
        means.append(win / cnt[None, :, None])
    return jnp.concatenate(means, axis=-1), full[:, T:]


def diff_attn_block(q, k, v, q_pos, k_pos, lam):
    s = jnp.einsum('bqhcd,bkhcd->bhcqk', q, k).astype(jnp.float32) * (D_DIFF ** -0.5)
    visible = (k_pos[None, :] // CHUNK) <= (q_pos[:, None] // CHUNK)
    p = jax.nn.softmax(jnp.where(visible, s, -jnp.inf), axis=-1)
    a = p[:, :, 0] - lam * p[:, :, 1]
    return jnp.einsum('bhqk,bkhe->bqhe', a, v.astype(jnp.float32))


def diff_attention(q, k, v, q_pos, k_pos, lam):
    B, Tq = q.shape[0], q.shape[1]
    if Tq <= Q_BLOCK:
        return diff_attn_block(q, k, v, q_pos, k_pos, lam)
    nb = Tq // Q_BLOCK
    qb = q.reshape((B, nb, Q_BLOCK) + q.shape[2:]).swapaxes(0, 1)
    pb = q_pos.reshape(nb, Q_BLOCK)
    out = lax.map(lambda a: diff_attn_block(a[0], k, v, a[1], k_pos, lam), (qb, pb))
    return out.swapaxes(0, 1).reshape(B, Tq, H_D, 2 * D_DIFF)


def even_mixer(h, pos, w_in, i_bias, f_bias, m_norm_g, r_norm_g, w_out, C0, n0, m0, S0):
    B, T, _ = h.shape
    f32 = jnp.float32
    mq, mk, mv, mo, mi, mf, rq, rk, rv, rg = jnp.split(h @ w_in, EVEN_SPLITS, axis=-1)
    hd = lambda a, d: a.reshape(B, T, -1, d).astype(f32)
    ig = (mi + i_bias).astype(f32)
    lf = jax.nn.log_sigmoid((mf + f_bias).astype(f32))
    hm, C, n, m = mlstm_chunkwise(hd(mq, DH_M), hd(mk, DH_M), hd(mv, DH_M), ig, lf, C0.astype(f32), n0.astype(f32), m0.astype(f32))
    hm = jax.nn.sigmoid(mo.astype(f32)) * rms_norm(hm, m_norm_g).reshape(B, T, W_M)
    hr, S = retention_chunkwise(rope(hd(rq, DH_R), pos), rope(hd(rk, DH_R), pos), hd(rv, DH_R), S0.astype(f32))
    hr = jax.nn.silu(rg.astype(f32)) * rms_norm(hr, r_norm_g).reshape(B, T, W_R)
    out = jnp.concatenate([hm, hr], axis=-1).astype(h.dtype) @ w_out
    dt = h.dtype
    return out, (C.astype(dt), n.astype(dt), m.astype(dt), S.astype(dt))


def odd_mixer(h, pos, lam_init, w_in, p_w, p_scale, lq1, lk1, lq2, lk2, d_norm_g, w_out, pool_past, k_past, v_past):
    B, T, _ = h.shape
    f32 = jnp.float32
    pin, dq, dk, dv = jnp.split(h @ w_in, ODD_SPLITS, axis=-1)
    pooled, pool_new = multi_scale_pool(pin, pool_past, pos)
    dev = (pooled - pin.astype(f32)).reshape(B, T, N_POOL_GROUPS, GP)
    yp = jnp.einsum('btgc,gce->btge', dev, p_w.astype(f32)).reshape(B, T, W_P) * p_scale.astype(f32)
    q = rope(dq.reshape(B, T, H_D, 2, D_DIFF), pos)
    k_rows = rope(dk.reshape(B, T, H_D, 2, D_DIFF), pos).reshape(B, T, H_D, 2 * D_DIFF)
    v_rows = dv.reshape(B, T, H_D, 2 * D_DIFF)
    if k_past is None:
        keys, vals, k_pos = k_rows, v_rows, pos
    else:
        keys = jnp.concatenate([k_past.astype(k_rows.dtype), k_rows], axis=1)
        vals = jnp.concatenate([v_past.astype(v_rows.dtype), v_rows], axis=1)
        k_pos = jnp.concatenate([jnp.arange(k_past.shape[1], dtype=jnp.int32), pos])
    lam = (jnp.exp(jnp.sum(lq1.astype(f32) * lk1.astype(f32)))
           - jnp.exp(jnp.sum(lq2.astype(f32) * lk2.astype(f32))) + lam_init)
    o = diff_attention(q, keys.reshape(B, -1, H_D, 2, D_DIFF), vals, pos, k_pos, lam)
    o = rms_norm(o, d_norm_g) * (1.0 - lam_init)
    out = jnp.concatenate([yp, o.reshape(B, T, W_D).astype(f32)], axis=-1).astype(h.dtype) @ w_out
    return out, (pool_new, k_rows, v_rows)


def moe(h, router_w, router_b, w1, w3, w2):
    B, T, D = h.shape
    t = h.reshape(B * T, D)
    s = jax.nn.sigmoid((t @ router_w).astype(jnp.float32))
    sel = s + router_b.astype(jnp.float32)
    gscore = jnp.sum(lax.top_k(sel.reshape(-1, N_GROUPS, EPG), TOP_K)[0], axis=-1)
    gbest = jnp.argmax(gscore, axis=-1)
    in_g = (jnp.arange(N_EXPERTS) // EPG)[None, :] == gbest[:, None]
    _, idx = lax.top_k(jnp.where(in_g, sel, -jnp.inf), TOP_K)
    wts = jnp.take_along_axis(s, idx, axis=-1)
    wts = wts / jnp.sum(wts, axis=-1, keepdims=True)
    gates = jnp.einsum('nk,nke->ne', wts, jax.nn.one_hot(idx, N_EXPERTS, dtype=jnp.float32))
    out = jnp.zeros((B * T, D), jnp.float32)
    for e in range(N_EXPERTS):
        ff = (jax.nn.silu(t @ w1[e]) * (t @ w3[e])) @ w2[e]
        out = out + gates[:, e:e + 1] * ff.astype(jnp.float32)
    return out.reshape(B, T, D).astype(h.dtype)


def setup_inputs(seed: int = 0) -> dict:
    key = jax.random.key(seed)
    ks = iter(jax.random.split(key, 48))
    f32 = jnp.float32

    def nrm(shape, scale=1.0):
        return jax.random.normal(next(ks), shape, f32) * scale

    def gain(shape, s=0.02):
        return 1.0 + s * jax.random.normal(next(ks), shape, f32)

    return {
        'x_prompt': nrm((BATCH, SEQ, D_MODEL)),
        'x_sample': nrm((DEC_BATCH, DEC_SEQ, D_MODEL)),
        'c_prompt': nrm((BATCH, D_MODEL)),
        'c_sample': nrm((DEC_BATCH, D_MODEL)),
        'state_mlstm_c': nrm((N_EVEN, DEC_BATCH, H_M, DH_M, DH_M), 0.1),
        'state_mlstm_n': nrm((N_EVEN, DEC_BATCH, H_M, DH_M), 0.1),
        'state_mlstm_m': nrm((N_EVEN, DEC_BATCH, H_M)),
        'state_ret': nrm((N_EVEN, DEC_BATCH, H_R, DH_R, DH_R), 0.1),
        'state_pool': nrm((N_ODD, DEC_BATCH, POOL_MAX - 1, W_P)),
        'cache_k': nrm((N_ODD, DEC_BATCH, PAST_LEN, H_D, 2 * D_DIFF)),
        'cache_v': nrm((N_ODD, DEC_BATCH, PAST_LEN, H_D, 2 * D_DIFF)),
        'ada_w': nrm((DEPTH, D_MODEL, 6 * D_MODEL), 0.5 * D_MODEL ** -0.5),
        'ada_b': nrm((DEPTH, 6 * D_MODEL), 0.02),
        'norm_mix_g': gain((DEPTH, D_MODEL)),
        'norm_ffn_g': gain((DEPTH, D_MODEL)),
        'final_norm_g': gain((D_MODEL,)),
        'w_in_even': nrm((N_EVEN, D_MODEL, IN_EVEN), D_MODEL ** -0.5),
        'mlstm_i_bias': nrm((N_EVEN, H_M), 0.5),
        'mlstm_f_bias': jnp.linspace(3.0, 6.0, H_M, dtype=f32)[None, :] + nrm((N_EVEN, H_M), 0.1),
        'mlstm_norm_g': gain((N_EVEN, H_M, DH_M)),
        'ret_norm_g': gain((N_EVEN, H_R, DH_R)),
        'w_out_even': nrm((N_EVEN, W_M + W_R, D_MODEL), (W_M + W_R) ** -0.5),
        'w_in_odd': nrm((N_ODD, D_MODEL, IN_ODD), D_MODEL ** -0.5),
        'pool_w': nrm((N_ODD, N_POOL_GROUPS, GP, GP), GP ** -0.5),
        'pool_scale': gain((N_ODD, W_P), 0.1),
        'lambda_q1': nrm((N_ODD, D_DIFF), 0.1),
        'lambda_k1': nrm((N_ODD, D_DIFF), 0.1),
        'lambda_q2': nrm((N_ODD, D_DIFF), 0.1),
        'lambda_k2': nrm((N_ODD, D_DIFF), 0.1),
        'diff_norm_g': gain((N_ODD, 2 * D_DIFF)),
        'w_out_odd': nrm((N_ODD, W_P + W_D, D_MODEL), (W_P + W_D) ** -0.5),
        'router_w': nrm((D_MODEL, N_EXPERTS), D_MODEL ** -0.5),
        'router_b': nrm((N_EXPERTS,), 0.01),
        'moe_w1': nrm((DEPTH, N_EXPERTS, D_MODEL, D_FF_EXPERT), D_MODEL ** -0.5),
        'moe_w3': nrm((DEPTH, N_EXPERTS, D_MODEL, D_FF_EXPERT), D_MODEL ** -0.5),
        'moe_w2': nrm((DEPTH, N_EXPERTS, D_FF_EXPERT, D_MODEL), D_FF_EXPERT ** -0.5),
    }


def reference(x_prompt, x_sample, c_prompt, c_sample, state_mlstm_c, state_mlstm_n, state_mlstm_m, state_ret, state_pool, cache_k, cache_v, ada_w, ada_b, norm_mix_g, norm_ffn_g, final_norm_g, w_in_even, mlstm_i_bias, mlstm_f_bias, mlstm_norm_g, ret_norm_g, w_out_even, w_in_odd, pool_w, pool_scale, lambda_q1, lambda_k1, lambda_q2, lambda_k2, diff_norm_g, w_out_odd, router_w, router_b, moe_w1, moe_w3, moe_w2):
    def trunk(x, c, pos0, is_sample):
        B, T, _ = x.shape
        pos = pos0 + jnp.arange(T, dtype=jnp.int32)
        ev, od = [], []
        for layer in range(DEPTH):
            mod = (jax.nn.silu(c) @ ada_w[layer] + ada_b[layer])[:, None, :]
            sh1, sc1, g1, sh2, sc2, g2 = jnp.split(mod, 6, axis=-1)
            h = rms_norm(x, norm_mix_g[layer]) * (1.0 + sc1) + sh1
            i = layer // 2
            if layer % 2 == 0:
                if is_sample:
                    C0, n0, m0, S0 = state_mlstm_c[i], state_mlstm_n[i], state_mlstm_m[i], state_ret[i]
                else:
                    C0 = jnp.zeros((B, H_M, DH_M, DH_M), jnp.float32)
                    n0 = jnp.zeros((B, H_M, DH_M), jnp.float32)
                    m0 = jnp.zeros((B, H_M), jnp.float32)
                    S0 = jnp.zeros((B, H_R, DH_R, DH_R), jnp.float32)
                out, st = even_mixer(h, pos, w_in_even[i], mlstm_i_bias[i], mlstm_f_bias[i], mlstm_norm_g[i], ret_norm_g[i], w_out_even[i], C0, n0, m0, S0)
                ev.append(st)
            else:
                if is_sample:
                    pool_past, k_past, v_past = state_pool[i], cache_k[i], cache_v[i]
                else:
                    pool_past = jnp.zeros((B, POOL_MAX - 1, W_P), x.dtype)
                    k_past, v_past = None, None
                lam_init = 0.8 - 0.6 * math.exp(-0.3 * layer)
                out, st = odd_mixer(h, pos, lam_init, w_in_odd[i], pool_w[i], pool_scale[i], lambda_q1[i], lambda_k1[i], lambda_q2[i], lambda_k2[i], diff_norm_g[i], w_out_odd[i], pool_past, k_past, v_past)
                od.append(st)
            x = x + g1 * out
            h = rms_norm(x, norm_ffn_g[layer]) * (1.0 + sc2) + sh2
            x = x + g2 * moe(h, router_w, router_b, moe_w1[layer], moe_w3[layer], moe_w2[layer])
        return rms_norm(x, final_norm_g), ev, od

    y_prompt, ev_p, od_p = trunk(x_prompt, c_prompt, 0, False)
    y_sample, ev_s, od_s = trunk(x_sample, c_sample, PAST_LEN, True)
    p_mlstm_c = jnp.stack([s[0] for s in ev_p])
    p_mlstm_n = jnp.stack([s[1] for s in ev_p])
    p_mlstm_m = jnp.stack([s[2] for s in ev_p])
    p_ret = jnp.stack([s[3] for s in ev_p])
    p_pool = jnp.stack([s[0] for s in od_p])
    p_k = jnp.stack([s[1] for s in od_p])
    p_v = jnp.stack([s[2] for s in od_p])
    s_mlstm_c = jnp.stack([s[0] for s in ev_s])
    s_mlstm_n = jnp.stack([s[1] for s in ev_s])
    s_mlstm_m = jnp.stack([s[2] for s in ev_s])
    s_ret = jnp.stack([s[3] for s in ev_s])
    s_pool = jnp.stack([s[0] for s in od_s])
    s_k = jnp.stack([s[1] for s in od_s])
    s_v = jnp.stack([s[2] for s in od_s])
    return (y_prompt, y_sample, p_mlstm_c, p_mlstm_n, p_mlstm_m, p_ret, p_pool, p_k, p_v, s_mlstm_c, s_mlstm_n, s_mlstm_m, s_ret, s_pool, s_k, s_v)
```

```python
import functools
import math

import jax
import jax.numpy as jnp
from jax import lax
from jax.experimental import pallas as pl
from jax.experimental.pallas import tpu as pltpu

F32 = jnp.float32
BF16 = jnp.bfloat16

EPS = 1e-6
ROPE_THETA = 10000.0
MASK_CHUNK = 64
MASK_SHIFT = 6
DH = 128
N_HEADS = 4
D_DIFF = 64
POOL_WINDOWS = (2, 4, 8, 16)
POOL_PAD = 16
N_EXPERTS = 16
EPG = 4
NEG_BIG = -1e30
LANES = 128
VMEM_LIMIT = 52 * 1024 * 1024

_NT = (((1,), (1,)), ((), ()))
_TN = (((0,), (0,)), ((), ()))


def _mm(a, b, hi, dims=None):
    if hi:
        a, b, prec = a.astype(F32), b.astype(F32), lax.Precision.HIGHEST
    else:
        a, b, prec = a.astype(BF16), b.astype(BF16), None
    if dims is None:
        return jnp.dot(a, b, precision=prec, preferred_element_type=F32)
    return lax.dot_general(a, b, dims, precision=prec, preferred_element_type=F32)


def _cdt(hi):
    return F32 if hi else BF16


def _rms_mod(x, g, sc, sh):
    ms = jnp.mean(x * x, axis=-1, keepdims=True)
    return (x * lax.rsqrt(ms + EPS) * g) * (1.0 + sc) + sh


def _params(sem):
    return pltpu.CompilerParams(dimension_semantics=sem, vmem_limit_bytes=VMEM_LIMIT)


def _resident(shape):
    zeros = (0,) * len(shape)
    return pl.BlockSpec(shape, lambda *args: zeros, pipeline_mode=pl.Buffered(1))


def _adaln_kernel(c_ref, w_ref, b_ref, o_ref):
    c = c_ref[...]
    o_ref[0] = _mm(c * jax.nn.sigmoid(c), w_ref[0], True) + b_ref[0]


def _adaln(c_all, ada_w, ada_b):
    n_layers, d, n6 = ada_w.shape
    rows = c_all.shape[0]
    bn = 1536
    return pl.pallas_call(
        _adaln_kernel,
        grid=(n_layers, n6 // bn),
        in_specs=[
            pl.BlockSpec((rows, d), lambda l, j: (0, 0)),
            pl.BlockSpec((1, d, bn), lambda l, j: (l, 0, j)),
            pl.BlockSpec((1, 1, bn), lambda l, j: (l, 0, j)),
        ],
        out_specs=pl.BlockSpec((1, rows, bn), lambda l, j: (l, 0, j)),
        out_shape=jax.ShapeDtypeStruct((n_layers, rows, n6), F32),
        compiler_params=_params(("parallel", "parallel")),
        name="adaln",
    )(c_all, ada_w, ada_b.reshape(n_layers, 1, n6))


def _log_sigmoid(x):
    return jnp.minimum(x, 0.0) - jnp.log1p(jnp.exp(-jnp.abs(x)))


def _even_kernel(x_ref, sc_ref, sh_ref, g1_ref, ng_ref, wrow_ref, wkt_ref, wgt_ref, gb_ref,
                 cosr_ref, sinr_ref, cost_ref, sint_ref, mng_ref, rng_ref, wout_ref,
                 c0_ref, m0_ref, s0_ref,
                 xo_ref, cf_ref, mf_ref, sf_ref,
                 r_scr, kt_scr, c_scr, m_scr, s_scr, y_scr, *, lc, lv, hi):
    t = pl.program_id(1)
    n_t = pl.num_programs(1)
    mm = functools.partial(_mm, hi=hi)

    @pl.when(t == 0)
    def _():
        c_scr[...] = c0_ref[0]
        m_scr[...] = m0_ref[0]
        s_scr[...] = s0_ref[0]

    x = x_ref[0]
    h = _rms_mod(x, ng_ref[...], sc_ref[0], sh_ref[0]).astype(_cdt(hi))
    r_scr[...] = mm(h, wrow_ref[...])
    kt_scr[...] = mm(wkt_ref[...], h, dims=_NT) * (DH ** -0.5)
    gt = mm(wgt_ref[...], h, dims=_NT) + gb_ref[...]

    lane = lax.broadcasted_iota(jnp.int32, (1, lc), 1)
    valid = lane < lv
    row_i = lax.broadcasted_iota(jnp.int32, (lc, lc), 0)
    col_i = lax.broadcasted_iota(jnp.int32, (lc, lc), 1)
    tri = col_i <= row_i
    eye = col_i == row_i

    ig = gt[0:N_HEADS]
    lf = jnp.where(valid, _log_sigmoid(gt[N_HEADS:2 * N_HEADS]), 0.0)
    upper = (row_i <= col_i).astype(F32)
    b_all = _mm(lf, upper, True)
    u_all = jnp.where(valid, ig - b_all, -jnp.inf)

    ones_col = (lax.broadcasted_iota(jnp.int32, (lc, DH), 1) == 0).astype(_cdt(hi))

    for hd in range(N_HEADS):
        sl = slice(hd * DH, (hd + 1) * DH)
        q = r_scr[:, sl].astype(_cdt(hi))
        v = r_scr[:, 512 + hd * DH:512 + (hd + 1) * DH].astype(_cdt(hi))
        vaug = jnp.concatenate([v, ones_col], axis=1)
        kt = kt_scr[sl, :]
        u_row = u_all[hd:hd + 1]
        b_row = b_all[hd:hd + 1]
        m_prev = m_scr[hd]
        caug = c_scr[hd]

        u_mat = jnp.where(tri, u_row, -jnp.inf)
        r_col = jnp.maximum(m_prev, jnp.max(u_mat, axis=1, keepdims=True))
        b_col = jnp.sum(jnp.where(eye, b_row, 0.0), axis=1, keepdims=True)
        s_qk = mm(q, kt)
        p = jnp.exp(u_mat - r_col) * s_qk
        sc_col = jnp.exp(m_prev - r_col)
        nd = sc_col * mm(q, caug) + mm(p, vaug)
        num = nd[:, :DH]
        den = nd[:, DH:DH + 1]
        m_t = b_col + r_col
        h_m = num / jnp.maximum(jnp.abs(den), jnp.exp(-m_t))
        ms = jnp.mean(h_m * h_m, axis=-1, keepdims=True)
        h_m = h_m * lax.rsqrt(ms + EPS) * mng_ref[:, sl]
        gate = jax.nn.sigmoid(r_scr[:, 1024 + hd * DH:1024 + (hd + 1) * DH])
        y_scr[:, sl] = (gate * h_m).astype(y_scr.dtype)

        u_max = jnp.max(u_row, axis=1, keepdims=True)
        w_row = jnp.exp(u_row - u_max)
        kv = mm(kt * w_row, vaug)
        b_end = b_row[:, lc - 1:lc]
        a_max = b_end + u_max
        m_new = jnp.maximum(b_end + m_prev, a_max)
        c_scr[hd] = jnp.exp(b_end + m_prev - m_new) * caug + jnp.exp(a_max - m_new) * kv
        m_scr[hd] = m_new

    rel = (row_i - col_i).astype(F32)
    jrow = lane.astype(F32)
    jcol = lax.broadcasted_iota(jnp.int32, (lc, 1), 0).astype(F32)
    cos_r = cosr_ref[...]
    sin_r = sinr_ref[...]
    cos_t = cost_ref[...]
    sin_t = sint_ref[...]
    for hd in range(N_HEADS):
        lg = math.log(1.0 - 2.0 ** (-5.0 - hd))
        sl = slice(hd * DH, (hd + 1) * DH)
        rq = r_scr[:, 1536 + hd * DH:1536 + (hd + 1) * DH]
        rq = (rq * cos_r + pltpu.roll(rq, DH // 2, 1) * sin_r).astype(_cdt(hi))
        rv = r_scr[:, 2048 + hd * DH:2048 + (hd + 1) * DH].astype(_cdt(hi))
        kt = kt_scr[512 + hd * DH:512 + (hd + 1) * DH, :]
        top, bot = kt[:DH // 2], kt[DH // 2:]
        kt = jnp.concatenate([top * cos_t - bot * sin_t, top * sin_t + bot * cos_t], axis=0)
        s_prev = s_scr[hd]

        decay = jnp.where(rel >= 0, jnp.exp(lg * jnp.maximum(rel, 0.0)), 0.0)
        scores = mm(rq, kt) * decay
        intra = mm(scores, rv)
        inter = mm(rq, s_prev) * jnp.exp(lg * (jcol + 1.0))
        o = intra + inter
        ms = jnp.mean(o * o, axis=-1, keepdims=True)
        o = o * lax.rsqrt(ms + EPS) * rng_ref[:, sl]
        g_in = r_scr[:, 2560 + hd * DH:2560 + (hd + 1) * DH]
        y_scr[:, 512 + hd * DH:512 + (hd + 1) * DH] = (g_in * jax.nn.sigmoid(g_in) * o).astype(y_scr.dtype)

        w_end = jnp.where(valid, jnp.exp(lg * (lv - 1.0 - jrow)), 0.0)
        s_scr[hd] = math.exp(lg * lv) * s_prev + mm(kt * w_end, rv)

    xo_ref[0] = x + g1_ref[0] * mm(y_scr[...], wout_ref[...])

    @pl.when(t == n_t - 1)
    def _():
        cf_ref[0] = c_scr[...]
        mf_ref[0] = m_scr[...]
        sf_ref[0] = s_scr[...]


def _even_mixer(x, sc, sh, g1, norm_g, w_in, i_bias, f_bias, mnorm_g, rnorm_g, w_out,
                c0, n0, m0, s0, pos0, t_valid, hi):
    bsz, t_pad, d = x.shape
    lc = LANES
    n_t = t_pad // lc
    lv = lc if n_t > 1 else t_valid
    assert n_t == 1 or t_valid == t_pad
    wdt = _cdt(hi)
    w = w_in
    cols = lambda a, b: w[:, a:b]
    w_row = jnp.concatenate([cols(0, 512), cols(1024, 1536), cols(1536, 2048),
                             cols(2056, 2568), cols(3080, 3592), cols(3592, 4104)], axis=1).astype(wdt)
    w_kt = jnp.concatenate([cols(512, 1024), cols(2568, 3080)], axis=1).T.astype(wdt)
    w_gt = jnp.concatenate([cols(2048, 2056).T, jnp.zeros((8, d), F32)], axis=0).astype(wdt)
    gb = jnp.concatenate([i_bias, f_bias, jnp.zeros((8,), F32)]).reshape(16, 1)

    half = DH // 2
    inv = jnp.power(ROPE_THETA, -jnp.arange(half, dtype=F32) * 2.0 / DH)
    pos = pos0 + jnp.arange(t_pad, dtype=jnp.int32)
    ang = pos.astype(F32)[:, None] * inv[None, :]
    cos, sin = jnp.cos(ang), jnp.sin(ang)
    cos_r = jnp.concatenate([cos, cos], axis=1)
    sin_r = jnp.concatenate([-sin, sin], axis=1)
    cos_t, sin_t = cos.T, sin.T

    caug0 = jnp.concatenate([c0, n0[..., None], jnp.zeros(c0.shape[:-1] + (DH - 1,), F32)], axis=-1)
    m0 = m0.reshape(bsz, N_HEADS, 1, 1)

    full = lambda shape: _resident(shape)
    per_b = lambda shape: pl.BlockSpec((1,) + shape, lambda b, t: (b,) + (0,) * len(shape))
    outs = pl.pallas_call(
        functools.partial(_even_kernel, lc=lc, lv=lv, hi=hi),
        grid=(bsz, n_t),
        in_specs=[
            pl.BlockSpec((1, lc, d), lambda b, t: (b, t, 0)),
            per_b((1, d)), per_b((1, d)), per_b((1, d)),
            full((1, d)),
            full((d, 3072)), full((1024, d)), full((16, d)), full((16, 1)),
            pl.BlockSpec((lc, DH), lambda b, t: (t, 0)),
            pl.BlockSpec((lc, DH), lambda b, t: (t, 0)),
            pl.BlockSpec((half, lc), lambda b, t: (0, t)),
            pl.BlockSpec((half, lc), lambda b, t: (0, t)),
            full((1, 512)), full((1, 512)), full((d, d)),
            per_b((N_HEADS, DH, 2 * DH)), per_b((N_HEADS, 1, 1)), per_b((N_HEADS, DH, DH)),
        ],
        out_specs=[
            pl.BlockSpec((1, lc, d), lambda b, t: (b, t, 0)),
            per_b((N_HEADS, DH, 2 * DH)), per_b((N_HEADS, 1, 1)), per_b((N_HEADS, DH, DH)),
        ],
        out_shape=[
            jax.ShapeDtypeStruct((bsz, t_pad, d), F32),
            jax.ShapeDtypeStruct((bsz, N_HEADS, DH, 2 * DH), F32),
            jax.ShapeDtypeStruct((bsz, N_HEADS, 1, 1), F32),
            jax.ShapeDtypeStruct((bsz, N_HEADS, DH, DH), F32),
        ],
        scratch_shapes=[
            pltpu.VMEM((lc, 3072), F32),
            pltpu.VMEM((1024, lc), F32),
            pltpu.VMEM((N_HEADS, DH, 2 * DH), F32),
            pltpu.VMEM((N_HEADS, 1, 1), F32),
            pltpu.VMEM((N_HEADS, DH, DH), F32),
            pltpu.VMEM((lc, d), wdt),
        ],
        compiler_params=_params(("parallel", "arbitrary")),
        name="even_mixer_hi" if hi else "even_mixer",
    )(x, sc, sh, g1, norm_g.reshape(1, d), w_row, w_kt, w_gt, gb,
      cos_r, sin_r, cos_t, sin_t, mnorm_g.reshape(1, 512), rnorm_g.reshape(1, 512),
      w_out.astype(wdt), caug0, m0, s0)
    x_new, caug, m_f, s_f = outs
    return x_new, caug[..., :DH], caug[..., DH], m_f.reshape(bsz, N_HEADS), s_f


def _odd_proj_kernel(x_ref, sc_ref, sh_ref, ng_ref, w_ref, wvt_ref, cos_ref, sin_ref,
                     pw_ref, ps_ref, past_ref,
                     krow_ref, vrow_ref, kb_ref, q0_ref, q1_ref, vt_ref, yp_ref, pool_ref,
                     r_scr, ext_scr, *, tm, tv, pos0, hi):
    t = pl.program_id(1)
    mm = functools.partial(_mm, hi=hi)

    @pl.when(t == 0)
    def _():
        ext_scr[0:POOL_PAD] = past_ref[0]

    x = x_ref[0]
    h = _rms_mod(x, ng_ref[...], sc_ref[0], sh_ref[0]).astype(_cdt(hi))
    r_scr[...] = mm(h, w_ref[...])
    vt_ref[0] = mm(wvt_ref[...], h, dims=_NT).astype(vt_ref.dtype)
    vrow_ref[0] = r_scr[:, 1536:2048]

    ext_scr[POOL_PAD:POOL_PAD + tm] = r_scr[:, 0:512]
    pos = (pos0 + t * tm + lax.broadcasted_iota(jnp.int32, (tm, 1), 0)).astype(F32)
    for g, w in enumerate(POOL_WINDOWS):
        sl = slice(g * LANES, (g + 1) * LANES)
        pin = r_scr[:, sl]
        win = pin
        for j in range(1, w):
            win = win + ext_scr[POOL_PAD - j:POOL_PAD - j + tm, sl]
        cnt = jnp.minimum(pos + 1.0, float(w))
        dev = win / cnt - pin
        yp_ref[0, :, sl] = (mm(dev, pw_ref[g]) * ps_ref[:, sl]).astype(yp_ref.dtype)
    pool_ref[0] = ext_scr[tv:tv + POOL_PAD]
    ext_scr[0:POOL_PAD] = ext_scr[tm:tm + POOL_PAD]

    lane = lax.broadcasted_iota(jnp.int32, (1, LANES), 1)
    low_half = (lane & (D_DIFF - 1)) < (D_DIFF // 2)
    comp0 = lane < D_DIFF
    cos = cos_ref[...]
    sin = sin_ref[...]

    def rope(a):
        swapped = jnp.where(low_half, pltpu.roll(a, LANES - D_DIFF // 2, 1), pltpu.roll(a, D_DIFF // 2, 1))
        return a * cos + swapped * sin

    for hd in range(N_HEADS):
        sl = slice(hd * LANES, (hd + 1) * LANES)
        q = rope(r_scr[:, 512 + hd * LANES:512 + (hd + 1) * LANES]) * (D_DIFF ** -0.5)
        q0_ref[0, :, sl] = jnp.where(comp0, q, 0.0).astype(q0_ref.dtype)
        q1_ref[0, :, sl] = jnp.where(comp0, 0.0, q).astype(q1_ref.dtype)
        k = rope(r_scr[:, 1024 + hd * LANES:1024 + (hd + 1) * LANES])
        krow_ref[0, :, sl] = k
        kb_ref[0, :, sl] = k.astype(kb_ref.dtype)


def _odd_proj(x, sc, sh, norm_g, w_in, pool_w, pool_scale, pool_past, pos0, t_valid, tm, hi):
    bsz, t_pad, d = x.shape
    n_t = t_pad // tm
    tv = tm if n_t > 1 else t_valid
    assert n_t == 1 or t_valid == t_pad
    wdt = _cdt(hi)
    w_all = w_in.astype(wdt)
    w_vt = w_in[:, 1536:2048].T.astype(wdt)

    half = D_DIFF // 2
    inv = jnp.power(ROPE_THETA, -jnp.arange(half, dtype=F32) * 2.0 / D_DIFF)
    pos = pos0 + jnp.arange(t_pad, dtype=jnp.int32)
    ang = pos.astype(F32)[:, None] * inv[None, :]
    cos, sin = jnp.cos(ang), jnp.sin(ang)
    cos_t = jnp.concatenate([cos, cos, cos, cos], axis=1)
    sin_t = jnp.concatenate([-sin, sin, -sin, sin], axis=1)
    past = jnp.concatenate([jnp.zeros((bsz, 1, 512), F32), pool_past], axis=1)

    full = lambda shape: _resident(shape)
    per_b = lambda shape: pl.BlockSpec((1,) + shape, lambda b, t: (b,) + (0,) * len(shape))
    tile = lambda n: pl.BlockSpec((1, tm, n), lambda b, t: (b, t, 0))
    f32_rows = jax.ShapeDtypeStruct((bsz, t_pad, 512), F32)
    c_rows = jax.ShapeDtypeStruct((bsz, t_pad, 512), wdt)
    return pl.pallas_call(
        functools.partial(_odd_proj_kernel, tm=tm, tv=tv, pos0=pos0, hi=hi),
        grid=(bsz, n_t),
        in_specs=[
            tile(d), per_b((1, d)), per_b((1, d)), full((1, d)),
            full((d, 2048)), full((512, d)),
            pl.BlockSpec((tm, LANES), lambda b, t: (t, 0)),
            pl.BlockSpec((tm, LANES), lambda b, t: (t, 0)),
            full((4, LANES, LANES)), full((1, 512)), per_b((POOL_PAD, 512)),
        ],
        out_specs=[
            tile(512), tile(512), tile(512), tile(512), tile(512),
            pl.BlockSpec((1, 512, tm), lambda b, t: (b, 0, t)),
            tile(512), per_b((POOL_PAD, 512)),
        ],
        out_shape=[f32_rows, f32_rows, c_rows, c_rows, c_rows,
                   jax.ShapeDtypeStruct((bsz, 512, t_pad), wdt), c_rows,
                   jax.ShapeDtypeStruct((bsz, POOL_PAD, 512), F32)],
        scratch_shapes=[pltpu.VMEM((tm, 2048), F32), pltpu.VMEM((tm + POOL_PAD, 512), F32)],
        compiler_params=_params(("parallel", "arbitrary")),
        name="odd_proj_hi" if hi else "odd_proj",
    )(x, sc, sh, norm_g.reshape(1, d), w_all, w_vt, cos_t, sin_t,
      pool_w.astype(wdt), pool_scale.reshape(1, 512), past)


def _attn_kernel(qi_ref, kj_ref, last_ref,
                 q0_ref, q1_ref, kb_ref, vt_ref, x_ref, yp_ref, g1_ref, lamp_ref, dng_ref, wout_ref,
                 xo_ref, m_scr, l_scr, acc_scr, *, tq, tk, q_pos0, tk_valid, lam_init, hi):
    s = pl.program_id(1)
    qi = qi_ref[s]
    kj = kj_ref[s]
    mm = functools.partial(_mm, hi=hi)

    @pl.when(kj == 0)
    def _():
        m_scr[...] = jnp.full_like(m_scr, NEG_BIG)
        l_scr[...] = jnp.zeros_like(l_scr)
        acc_scr[...] = jnp.zeros_like(acc_scr)

    def step(masked):
        if masked:
            kpos = kj * tk + lax.broadcasted_iota(jnp.int32, (tk, tq), 0)
            qpos = q_pos0 + qi * tq + lax.broadcasted_iota(jnp.int32, (tk, tq), 1)
            visible = ((kpos >> MASK_SHIFT) <= (qpos >> MASK_SHIFT)) & (kpos < tk_valid)
        for hd in range(N_HEADS):
            sl = slice(hd * LANES, (hd + 1) * LANES)
            k = kb_ref[0, :, sl]
            vt = vt_ref[0, sl, :]
            for c, q_ref in enumerate((q0_ref, q1_ref)):
                i = 2 * hd + c
                st = mm(k, q_ref[0, :, sl], dims=_NT)
                if masked:
                    st = jnp.where(visible, st, NEG_BIG)
                m_prev = m_scr[i:i + 1]
                m_new = jnp.maximum(m_prev, jnp.max(st, axis=0, keepdims=True))
                alpha = jnp.exp(m_prev - m_new)
                p = jnp.exp(st - m_new)
                l_scr[i:i + 1] = alpha * l_scr[i:i + 1] + jnp.sum(p, axis=0, keepdims=True)
                acc_scr[i] = alpha * acc_scr[i] + mm(vt, p)
                m_scr[i:i + 1] = m_new

    fully_visible = ((kj + 1) * tk - 1) // MASK_CHUNK <= (q_pos0 + qi * tq) // MASK_CHUNK
    no_pad = (kj + 1) * tk <= tk_valid
    clean = jnp.logical_and(fully_visible, no_pad)
    pl.when(clean)(lambda: step(False))
    pl.when(jnp.logical_not(clean))(lambda: step(True))

    @pl.when(last_ref[s] == 1)
    def _():
        lp = lamp_ref[...]
        lam = (jnp.exp(jnp.sum(lp[0:1] * lp[1:2], axis=1, keepdims=True))
               - jnp.exp(jnp.sum(lp[2:3] * lp[3:4], axis=1, keepdims=True)) + lam_init)
        y = mm(yp_ref[0], wout_ref[0:512, :])
        for hd in range(N_HEADS):
            o0 = acc_scr[2 * hd] / l_scr[2 * hd:2 * hd + 1]
            o1 = acc_scr[2 * hd + 1] / l_scr[2 * hd + 1:2 * hd + 2]
            ot = o0 - lam * o1
            ms = jnp.mean(ot * ot, axis=0, keepdims=True)
            ot = (ot * lax.rsqrt(ms + EPS) * dng_ref[...]) * (1.0 - lam_init)
            y = y + mm(ot, wout_ref[512 + hd * LANES:512 + (hd + 1) * LANES, :], dims=_TN)
        xo_ref[0] = x_ref[0] + g1_ref[0] * y


def _attention(q0, q1, kb, vt, x, yp, g1, lam_params, dnorm_g, w_out, q_pos0, tk_valid, lam_init, tq, tk, hi):
    bsz, t_q, d = x.shape
    t_k = kb.shape[1]
    n_q, n_k = t_q // tq, t_k // tk
    qi, kj, last = [], [], []
    for i in range(n_q):
        q_chunk_last = (q_pos0 + (i + 1) * tq - 1) // MASK_CHUNK
        k_last = min((q_chunk_last + 1) * MASK_CHUNK, tk_valid) - 1
        j_last = min(k_last // tk, n_k - 1)
        for j in range(j_last + 1):
            qi.append(i)
            kj.append(j)
            last.append(1 if j == j_last else 0)
    qi, kj, last = (jnp.asarray(a, jnp.int32) for a in (qi, kj, last))
    n_steps = int(qi.shape[0])

    q_spec = pl.BlockSpec((1, tq, 512), lambda b, s, qi, kj, la: (b, qi[s], 0))
    full = lambda shape: _resident(shape)
    grid_spec = pltpu.PrefetchScalarGridSpec(
        num_scalar_prefetch=3,
        grid=(bsz, n_steps),
        in_specs=[
            q_spec, q_spec,
            pl.BlockSpec((1, tk, 512), lambda b, s, qi, kj, la: (b, kj[s], 0)),
            pl.BlockSpec((1, 512, tk), lambda b, s, qi, kj, la: (b, 0, kj[s])),
            pl.BlockSpec((1, tq, d), lambda b, s, qi, kj, la: (b, qi[s], 0)),
            q_spec,
            pl.BlockSpec((1, 1, d), lambda b, s, qi, kj, la: (b, 0, 0)),
            full((4, D_DIFF)), full((LANES, 1)), full((d, d)),
        ],
        out_specs=pl.BlockSpec((1, tq, d), lambda b, s, qi, kj, la: (b, qi[s], 0)),
        scratch_shapes=[
            pltpu.VMEM((2 * N_HEADS, tq), F32),
            pltpu.VMEM((2 * N_HEADS, tq), F32),
            pltpu.VMEM((2 * N_HEADS, LANES, tq), F32),
        ],
    )
    return pl.pallas_call(
        functools.partial(_attn_kernel, tq=tq, tk=tk, q_pos0=q_pos0, tk_valid=tk_valid,
                          lam_init=lam_init, hi=hi),
        grid_spec=grid_spec,
        out_shape=jax.ShapeDtypeStruct((bsz, t_q, d), F32),
        compiler_params=_params(("parallel", "arbitrary")),
        name="diff_attention_hi" if hi else "diff_attention",
    )(qi, kj, last, q0, q1, kb, vt, x, yp, g1, lam_params, dnorm_g.reshape(LANES, 1),
      w_out.astype(_cdt(hi)))


def _route(h, rw, rb):
    s = jax.nn.sigmoid(_mm(h, rw, True))
    sel = s + rb
    lane_i = lax.broadcasted_iota(jnp.int32, sel.shape, 1)
    lane = lane_i.astype(F32)
    group = (lane_i >> 2).astype(F32)

    def top2(vals):
        m1 = jnp.max(vals, axis=1, keepdims=True)
        i1 = jnp.min(jnp.where(vals == m1, lane, float(N_EXPERTS)), axis=1, keepdims=True)
        rest = jnp.where(lane == i1, -jnp.inf, vals)
        m2 = jnp.max(rest, axis=1, keepdims=True)
        i2 = jnp.min(jnp.where(rest == m2, lane, float(N_EXPERTS)), axis=1, keepdims=True)
        return m1, i1, m2, i2

    best = None
    for g in range(N_EXPERTS // EPG):
        m1, _, m2, _ = top2(jnp.where(group == float(g), sel, -jnp.inf))
        score = m1 + m2
        if best is None:
            best, gbest = score, jnp.zeros_like(score)
        else:
            better = score > best
            best = jnp.where(better, score, best)
            gbest = jnp.where(better, float(g), gbest)
    _, i1, _, i2 = top2(jnp.where(group == gbest, sel, -jnp.inf))
    w1 = jnp.sum(jnp.where(lane == i1, s, 0.0), axis=1, keepdims=True)
    w2 = jnp.sum(jnp.where(lane == i2, s, 0.0), axis=1, keepdims=True)
    tot = w1 + w2
    return jnp.where(lane == i1, w1 / tot, 0.0) + jnp.where(lane == i2, w2 / tot, 0.0)


def _moe_kernel(x_ref, sc_ref, sh_ref, g2_ref, ng_ref, rw_ref, rb_ref, w1_ref, w3_ref, w2_ref, fg_ref,
                o_ref, h_scr, gate_scr, acc_scr, *, final_norm, hi):
    e = pl.program_id(1)
    mm = functools.partial(_mm, hi=hi)

    @pl.when(e == 0)
    def _():
        h = _rms_mod(x_ref[0], ng_ref[...], sc_ref[0], sh_ref[0])
        h_scr[...] = h.astype(h_scr.dtype)
        gate_scr[...] = _route(h, rw_ref[...], rb_ref[...])
        acc_scr[...] = jnp.zeros_like(acc_scr)

    hb = h_scr[...]
    a = mm(hb, w1_ref[0])
    b = mm(hb, w3_ref[0])
    ff = mm(a * jax.nn.sigmoid(a) * b, w2_ref[0])
    lane = lax.broadcasted_iota(jnp.int32, gate_scr.shape, 1)
    gate = jnp.sum(jnp.where(lane == e, gate_scr[...], 0.0), axis=1, keepdims=True)
    acc_scr[...] += gate * ff

    @pl.when(e == N_EXPERTS - 1)
    def _():
        y = x_ref[0] + g2_ref[0] * acc_scr[...]
        if final_norm:
            ms = jnp.mean(y * y, axis=-1, keepdims=True)
            y = y * lax.rsqrt(ms + EPS) * fg_ref[...]
        o_ref[0] = y


def _moe(x, sc, sh, g2, norm_g, router_w, router_b, w1, w3, w2, final_g, final_norm, tm, hi):
    bsz, t_len, d = x.shape
    n_t = t_len // tm
    rows = sc.shape[1]
    if rows == 1:
        mod_spec = pl.BlockSpec((1, 1, d), lambda i, e: (i // n_t, 0, 0))
    else:
        mod_spec = pl.BlockSpec((1, tm, d), lambda i, e: (i // n_t, i % n_t, 0))
    x_spec = pl.BlockSpec((1, tm, d), lambda i, e: (i // n_t, i % n_t, 0))
    full = lambda shape: _resident(shape)
    dff = w1.shape[-1]
    return pl.pallas_call(
        functools.partial(_moe_kernel, final_norm=final_norm, hi=hi),
        grid=(bsz * n_t, N_EXPERTS),
        in_specs=[
            x_spec, mod_spec, mod_spec, mod_spec, full((1, d)),
            full((d, N_EXPERTS)), full((1, N_EXPERTS)),
            pl.BlockSpec((1, d, dff), lambda i, e: (e, 0, 0)),
            pl.BlockSpec((1, d, dff), lambda i, e: (e, 0, 0)),
            pl.BlockSpec((1, dff, d), lambda i, e: (e, 0, 0)),
            full((1, d)),
        ],
        out_specs=x_spec,
        out_shape=jax.ShapeDtypeStruct((bsz, t_len, d), F32),
        scratch_shapes=[
            pltpu.VMEM((tm, d), _cdt(hi)),
            pltpu.VMEM((tm, N_EXPERTS), F32),
            pltpu.VMEM((tm, d), F32),
        ],
        compiler_params=_params(("parallel", "arbitrary")),
        name="moe_hi" if hi else "moe",
    )(x, sc, sh, g2, norm_g.reshape(1, d), router_w, router_b.reshape(1, N_EXPERTS), w1, w3, w2,
      final_g.reshape(1, d))


def _trunk(x, mods, pos0, t_valid, states, params, moe_weights, flat_moe, attn_tiles, proj_tm, moe_tm, hi):
    (norm_mix_g, norm_ffn_g, final_norm_g, w_in_even, i_bias, f_bias, mnorm_g, rnorm_g, w_out_even,
     w_in_odd, pool_w, pool_scale, lq1, lk1, lq2, lk2, dnorm_g, w_out_odd, router_w, router_b) = params
    c0, n0, m0, s0, pool_past, k_past, v_past = states
    bsz, t_pad, d = x.shape
    w1, w3, w2 = moe_weights

    def run_moe(x, layer, final):
        sc2, sh2, g2 = mods[layer][4], mods[layer][3], mods[layer][5]
        if flat_moe:
            n_rows = bsz * t_valid
            flat = lambda a: jnp.broadcast_to(a, (bsz, t_valid, d)).reshape(1, n_rows, d)
            y = _moe(x[:, :t_valid].reshape(1, n_rows, d), flat(sc2), flat(sh2), flat(g2), norm_ffn_g[layer],
                     router_w, router_b, w1[layer], w3[layer], w2[layer], final_norm_g, final, n_rows, hi)
            y = y.reshape(bsz, t_valid, d)
            return jnp.concatenate([y, jnp.zeros((bsz, t_pad - t_valid, d), F32)], axis=1)
        return _moe(x, sc2, sh2, g2, norm_ffn_g[layer], router_w, router_b,
                    w1[layer], w3[layer], w2[layer], final_norm_g, final, moe_tm, hi)

    sh1, sc1, g1 = mods[0][0], mods[0][1], mods[0][2]
    x, c_f, n_f, m_f, s_f = _even_mixer(x, sc1, sh1, g1, norm_mix_g[0], w_in_even[0], i_bias[0], f_bias[0],
                                        mnorm_g[0], rnorm_g[0], w_out_even[0], c0, n0, m0, s0, pos0, t_valid, hi)
    x = run_moe(x, 0, False)

    layer = 1
    lam_init = 0.8 - 0.6 * math.exp(-0.3 * layer)
    sh1, sc1, g1 = mods[1][0], mods[1][1], mods[1][2]
    k_rows, v_rows, kb, q0, q1, vt, yp, pool_new = _odd_proj(
        x, sc1, sh1, norm_mix_g[1], w_in_odd[0], pool_w[0], pool_scale[0], pool_past, pos0, t_valid, proj_tm, hi)
    if k_past is None:
        tk_valid = t_valid
    else:
        past_len = k_past.shape[1]
        tk_valid = past_len + t_valid
        kb = jnp.concatenate([k_past.reshape(bsz, past_len, 512).astype(kb.dtype), kb], axis=1)
        vt = jnp.concatenate([jnp.swapaxes(v_past.reshape(bsz, past_len, 512), 1, 2).astype(vt.dtype), vt], axis=2)
    lam_params = jnp.stack([lq1[0], lk1[0], lq2[0], lk2[0]])
    tq, tk = attn_tiles
    x = _attention(q0, q1, kb, vt, x, yp, g1, lam_params, dnorm_g[0], w_out_odd[0],
                   pos0, tk_valid, lam_init, tq, tk, hi)
    x = run_moe(x, 1, True)
    return x, (c_f, n_f, m_f, s_f), (pool_new[:, 1:], k_rows[:, :t_valid], v_rows[:, :t_valid])


def kernel(x_prompt, x_sample, c_prompt, c_sample, state_mlstm_c, state_mlstm_n, state_mlstm_m, state_ret, state_pool, cache_k, cache_v, ada_w, ada_b, norm_mix_g, norm_ffn_g, final_norm_g, w_in_even, mlstm_i_bias, mlstm_f_bias, mlstm_norm_g, ret_norm_g, w_out_even, w_in_odd, pool_w, pool_scale, lambda_q1, lambda_k1, lambda_q2, lambda_k2, diff_norm_g, w_out_odd, router_w, router_b, moe_w1, moe_w3, moe_w2):
    bp, tp, d = x_prompt.shape
    bs, ts, _ = x_sample.shape
    past_len = cache_k.shape[2]
    n_layers = ada_w.shape[0]

    c_all = jnp.concatenate([c_prompt, c_sample, jnp.zeros((16 - bp - bs, d), F32)], axis=0)
    mod_all = _adaln(c_all, ada_w, ada_b)

    def mods_for(lo, stop):
        return [[mod_all[l, lo:stop, None, k * d:(k + 1) * d] for k in range(6)] for l in range(n_layers)]

    params = (norm_mix_g, norm_ffn_g, final_norm_g, w_in_even, mlstm_i_bias, mlstm_f_bias,
              mlstm_norm_g.reshape(-1, 512), ret_norm_g.reshape(-1, 512), w_out_even,
              w_in_odd, pool_w, pool_scale, lambda_q1, lambda_k1, lambda_q2, lambda_k2, diff_norm_g,
              w_out_odd, router_w, router_b)

    zeros = lambda *s: jnp.zeros(s, F32)
    p_states = (zeros(bp, N_HEADS, DH, DH), zeros(bp, N_HEADS, DH), zeros(bp, N_HEADS),
                zeros(bp, N_HEADS, DH, DH), zeros(bp, POOL_PAD - 1, 512), None, None)
    moe_bf16 = (moe_w1.astype(BF16), moe_w3.astype(BF16), moe_w2.astype(BF16))
    y_p, ev_p, od_p = _trunk(x_prompt, mods_for(0, bp), 0, tp, p_states, params, moe_bf16,
                             flat_moe=False, attn_tiles=(min(512, tp), min(512, tp)), proj_tm=min(512, tp),
                             moe_tm=min(1024, tp), hi=False)

    ts_pad = LANES
    x_s = jnp.concatenate([x_sample, zeros(bs, ts_pad - ts, d)], axis=1)
    s_states = (state_mlstm_c[0], state_mlstm_n[0], state_mlstm_m[0], state_ret[0], state_pool[0],
                cache_k[0], cache_v[0])
    y_s, ev_s, od_s = _trunk(x_s, mods_for(bp, bp + bs), past_len, ts, s_states, params,
                             (moe_w1, moe_w3, moe_w2), flat_moe=True,
                             attn_tiles=(ts_pad, past_len + ts_pad), proj_tm=ts_pad, moe_tm=None, hi=True)
    y_s = y_s[:, :ts]

    hd4 = lambda a: a.reshape(a.shape[0], a.shape[1], N_HEADS, DH)
    return (y_p, y_s,
            ev_p[0][None], ev_p[1][None], ev_p[2][None], ev_p[3][None],
            od_p[0][None], hd4(od_p[1])[None], hd4(od_p[2])[None],
            ev_s[0][None], ev_s[1][None], ev_s[2][None], ev_s[3][None],
            od_s[0][None], hd4(od_s[1])[None], hd4(od_s[2])[None])
```

```python
import functools
import math

import jax
import jax.numpy as jnp
from jax import lax
from jax.experimental import pallas as pl
from jax.experimental.pallas import tpu as pltpu

F32 = jnp.float32
BF16 = jnp.bfloat16

EPS = 1e-6
ROPE_THETA = 10000.0
MASK_CHUNK = 64
MASK_SHIFT = 6
DH = 128
N_HEADS = 4
D_DIFF = 64
POOL_WINDOWS = (2, 4, 8, 16)
POOL_PAD = 16
N_EXPERTS = 16
EPG = 4
NEG_BIG = -1e30
LANES = 128
VT_ROWS = DH + 16
LOG2E = 1.4426950408889634
VMEM_LIMIT = 52 * 1024 * 1024

_NT = (((1,), (1,)), ((), ()))
_TN = (((0,), (0,)), ((), ()))


def _mm(a, b, hi, dims=None):
    if hi:
        a, b, prec = a.astype(F32), b.astype(F32), lax.Precision.HIGHEST
    else:
        a, b, prec = a.astype(BF16), b.astype(BF16), None
    if dims is None:
        return jnp.dot(a, b, precision=prec, preferred_element_type=F32)
    return lax.dot_general(a, b, dims, precision=prec, preferred_element_type=F32)


def _cdt(hi):
    return F32 if hi else BF16


def _rms_mod(x, g, sc, sh):
    ms = jnp.mean(x * x, axis=-1, keepdims=True)
    return (x * lax.rsqrt(ms + EPS) * g) * (1.0 + sc) + sh


def _params(sem):
    return pltpu.CompilerParams(dimension_semantics=sem, vmem_limit_bytes=VMEM_LIMIT)


def _resident(shape):
    zeros = (0,) * len(shape)
    return pl.BlockSpec(shape, lambda *args: zeros, pipeline_mode=pl.Buffered(1))


def _adaln_kernel(c_ref, w_ref, b_ref, o_ref):
    c = c_ref[...]
    o_ref[0] = _mm(c * jax.nn.sigmoid(c), w_ref[0], True) + b_ref[0]


def _adaln(c_all, ada_w, ada_b):
    n_layers, d, n6 = ada_w.shape
    rows = c_all.shape[0]
    bn = 1536
    return pl.pallas_call(
        _adaln_kernel,
        grid=(n_layers, n6 // bn),
        in_specs=[
            pl.BlockSpec((rows, d), lambda l, j: (0, 0)),
            pl.BlockSpec((1, d, bn), lambda l, j: (l, 0, j)),
            pl.BlockSpec((1, 1, bn), lambda l, j: (l, 0, j)),
        ],
        out_specs=pl.BlockSpec((1, rows, bn), lambda l, j: (l, 0, j)),
        out_shape=jax.ShapeDtypeStruct((n_layers, rows, n6), F32),
        compiler_params=_params(("parallel", "parallel")),
        name="adaln",
    )(c_all, ada_w, ada_b.reshape(n_layers, 1, n6))


def _log_sigmoid(x):
    return jnp.minimum(x, 0.0) - jnp.log1p(jnp.exp(-jnp.abs(x)))


def _even_kernel(x_ref, sc_ref, sh_ref, g1_ref, ng_ref, wrow_ref, wkt_ref, wgt_ref, gb_ref,
                 cosr_ref, sinr_ref, cost_ref, sint_ref, mng_ref, rng_ref, wout_ref,
                 c0_ref, m0_ref, s0_ref,
                 xo_ref, cf_ref, mf_ref, sf_ref,
                 r_scr, kt_scr, c_scr, m_scr, s_scr, y_scr, *, lc, lv, hi):
    t = pl.program_id(1)
    n_t = pl.num_programs(1)
    mm = functools.partial(_mm, hi=hi)

    @pl.when(t == 0)
    def _():
        c_scr[...] = c0_ref[0]
        m_scr[...] = m0_ref[0]
        s_scr[...] = s0_ref[0]

    x = x_ref[0]
    h = _rms_mod(x, ng_ref[...], sc_ref[0], sh_ref[0]).astype(_cdt(hi))
    r_scr[...] = mm(h, wrow_ref[...])
    kt_scr[...] = mm(wkt_ref[...], h, dims=_NT) * (DH ** -0.5)
    gt = mm(wgt_ref[...], h, dims=_NT) + gb_ref[...]

    lane = lax.broadcasted_iota(jnp.int32, (1, lc), 1)
    valid = lane < lv
    row_i = lax.broadcasted_iota(jnp.int32, (lc, lc), 0)
    col_i = lax.broadcasted_iota(jnp.int32, (lc, lc), 1)
    tri = col_i <= row_i
    eye = col_i == row_i

    ig = gt[0:N_HEADS]
    lf = jnp.where(valid, _log_sigmoid(gt[N_HEADS:2 * N_HEADS]), 0.0)
    upper = (row_i <= col_i).astype(F32)
    b_all = _mm(lf, upper, True)
    u_all = jnp.where(valid, ig - b_all, -jnp.inf)

    ones_col = (lax.broadcasted_iota(jnp.int32, (lc, DH), 1) == 0).astype(_cdt(hi))

    for hd in range(N_HEADS):
        sl = slice(hd * DH, (hd + 1) * DH)
        q = r_scr[:, sl].astype(_cdt(hi))
        v = r_scr[:, 512 + hd * DH:512 + (hd + 1) * DH].astype(_cdt(hi))
        vaug = jnp.concatenate([v, ones_col], axis=1)
        kt = kt_scr[sl, :]
        u_row = u_all[hd:hd + 1]
        b_row = b_all[hd:hd + 1]
        m_prev = m_scr[hd]
        caug = c_scr[hd]

        u_mat = jnp.where(tri, u_row, -jnp.inf)
        r_col = jnp.maximum(m_prev, jnp.max(u_mat, axis=1, keepdims=True))
        b_col = jnp.sum(jnp.where(eye, b_row, 0.0), axis=1, keepdims=True)
        s_qk = mm(q, kt)
        p = jnp.exp(u_mat - r_col) * s_qk
        sc_col = jnp.exp(m_prev - r_col)
        nd = sc_col * mm(q, caug) + mm(p, vaug)
        num = nd[:, :DH]
        den = nd[:, DH:DH + 1]
        m_t = b_col + r_col
        h_m = num / jnp.maximum(jnp.abs(den), jnp.exp(-m_t))
        ms = jnp.mean(h_m * h_m, axis=-1, keepdims=True)
        h_m = h_m * lax.rsqrt(ms + EPS) * mng_ref[:, sl]
        gate = jax.nn.sigmoid(r_scr[:, 1024 + hd * DH:1024 + (hd + 1) * DH])
        y_scr[:, sl] = (gate * h_m).astype(y_scr.dtype)

        u_max = jnp.max(u_row, axis=1, keepdims=True)
        w_row = jnp.exp(u_row - u_max)
        kv = mm(kt * w_row, vaug)
        b_end = b_row[:, lc - 1:lc]
        a_max = b_end + u_max
        m_new = jnp.maximum(b_end + m_prev, a_max)
        c_scr[hd] = jnp.exp(b_end + m_prev - m_new) * caug + jnp.exp(a_max - m_new) * kv
        m_scr[hd] = m_new

    rel = (row_i - col_i).astype(F32)
    jrow = lane.astype(F32)
    jcol = lax.broadcasted_iota(jnp.int32, (lc, 1), 0).astype(F32)
    cos_r = cosr_ref[...]
    sin_r = sinr_ref[...]
    cos_t = cost_ref[...]
    sin_t = sint_ref[...]
    for hd in range(N_HEADS):
        lg = math.log(1.0 - 2.0 ** (-5.0 - hd))
        sl = slice(hd * DH, (hd + 1) * DH)
        rq = r_scr[:, 1536 + hd * DH:1536 + (hd + 1) * DH]
        rq = (rq * cos_r + pltpu.roll(rq, DH // 2, 1) * sin_r).astype(_cdt(hi))
        rv = r_scr[:, 2048 + hd * DH:2048 + (hd + 1) * DH].astype(_cdt(hi))
        kt = kt_scr[512 + hd * DH:512 + (hd + 1) * DH, :]
        top, bot = kt[:DH // 2], kt[DH // 2:]
        kt = jnp.concatenate([top * cos_t - bot * sin_t, top * sin_t + bot * cos_t], axis=0)
        s_prev = s_scr[hd]

        decay = jnp.where(rel >= 0, jnp.exp(lg * jnp.maximum(rel, 0.0)), 0.0)
        scores = mm(rq, kt) * decay
        intra = mm(scores, rv)
        inter = mm(rq, s_prev) * jnp.exp(lg * (jcol + 1.0))
        o = intra + inter
        ms = jnp.mean(o * o, axis=-1, keepdims=True)
        o = o * lax.rsqrt(ms + EPS) * rng_ref[:, sl]
        g_in = r_scr[:, 2560 + hd * DH:2560 + (hd + 1) * DH]
        y_scr[:, 512 + hd * DH:512 + (hd + 1) * DH] = (g_in * jax.nn.sigmoid(g_in) * o).astype(y_scr.dtype)

        w_end = jnp.where(valid, jnp.exp(lg * (lv - 1.0 - jrow)), 0.0)
        s_scr[hd] = math.exp(lg * lv) * s_prev + mm(kt * w_end, rv)

    xo_ref[0] = x + g1_ref[0] * mm(y_scr[...], wout_ref[...])

    @pl.when(t == n_t - 1)
    def _():
        cf_ref[0] = c_scr[...]
        mf_ref[0] = m_scr[...]
        sf_ref[0] = s_scr[...]


def _even_mixer(x, sc, sh, g1, norm_g, w_in, i_bias, f_bias, mnorm_g, rnorm_g, w_out,
                c0, n0, m0, s0, pos0, t_valid, hi):
    bsz, t_pad, d = x.shape
    lc = LANES
    n_t = t_pad // lc
    lv = lc if n_t > 1 else t_valid
    assert n_t == 1 or t_valid == t_pad
    wdt = _cdt(hi)
    w = w_in
    cols = lambda a, b: w[:, a:b]
    w_row = jnp.concatenate([cols(0, 512), cols(1024, 1536), cols(1536, 2048),
                             cols(2056, 2568), cols(3080, 3592), cols(3592, 4104)], axis=1).astype(wdt)
    w_kt = jnp.concatenate([cols(512, 1024), cols(2568, 3080)], axis=1).T.astype(wdt)
    w_gt = jnp.concatenate([cols(2048, 2056).T, jnp.zeros((8, d), F32)], axis=0).astype(wdt)
    gb = jnp.concatenate([i_bias, f_bias, jnp.zeros((8,), F32)]).reshape(16, 1)

    half = DH // 2
    inv = jnp.power(ROPE_THETA, -jnp.arange(half, dtype=F32) * 2.0 / DH)
    pos = pos0 + jnp.arange(t_pad, dtype=jnp.int32)
    ang = pos.astype(F32)[:, None] * inv[None, :]
    cos, sin = jnp.cos(ang), jnp.sin(ang)
    cos_r = jnp.concatenate([cos, cos], axis=1)
    sin_r = jnp.concatenate([-sin, sin], axis=1)
    cos_t, sin_t = cos.T, sin.T

    caug0 = jnp.concatenate([c0, n0[..., None], jnp.zeros(c0.shape[:-1] + (DH - 1,), F32)], axis=-1)
    m0 = m0.reshape(bsz, N_HEADS, 1, 1)

    full = lambda shape: _resident(shape)
    per_b = lambda shape: pl.BlockSpec((1,) + shape, lambda b, t: (b,) + (0,) * len(shape))
    outs = pl.pallas_call(
        functools.partial(_even_kernel, lc=lc, lv=lv, hi=hi),
        grid=(bsz, n_t),
        in_specs=[
            pl.BlockSpec((1, lc, d), lambda b, t: (b, t, 0)),
            per_b((1, d)), per_b((1, d)), per_b((1, d)),
            full((1, d)),
            full((d, 3072)), full((1024, d)), full((16, d)), full((16, 1)),
            pl.BlockSpec((lc, DH), lambda b, t: (t, 0)),
            pl.BlockSpec((lc, DH), lambda b, t: (t, 0)),
            pl.BlockSpec((half, lc), lambda b, t: (0, t)),
            pl.BlockSpec((half, lc), lambda b, t: (0, t)),
            full((1, 512)), full((1, 512)), full((d, d)),
            per_b((N_HEADS, DH, 2 * DH)), per_b((N_HEADS, 1, 1)), per_b((N_HEADS, DH, DH)),
        ],
        out_specs=[
            pl.BlockSpec((1, lc, d), lambda b, t: (b, t, 0)),
            per_b((N_HEADS, DH, 2 * DH)), per_b((N_HEADS, 1, 1)), per_b((N_HEADS, DH, DH)),
        ],
        out_shape=[
            jax.ShapeDtypeStruct((bsz, t_pad, d), F32),
            jax.ShapeDtypeStruct((bsz, N_HEADS, DH, 2 * DH), F32),
            jax.ShapeDtypeStruct((bsz, N_HEADS, 1, 1), F32),
            jax.ShapeDtypeStruct((bsz, N_HEADS, DH, DH), F32),
        ],
        scratch_shapes=[
            pltpu.VMEM((lc, 3072), F32),
            pltpu.VMEM((1024, lc), F32),
            pltpu.VMEM((N_HEADS, DH, 2 * DH), F32),
            pltpu.VMEM((N_HEADS, 1, 1), F32),
            pltpu.VMEM((N_HEADS, DH, DH), F32),
            pltpu.VMEM((lc, d), wdt),
        ],
        compiler_params=_params(("parallel", "arbitrary")),
        name="even_mixer_hi" if hi else "even_mixer",
    )(x, sc, sh, g1, norm_g.reshape(1, d), w_row, w_kt, w_gt, gb,
      cos_r, sin_r, cos_t, sin_t, mnorm_g.reshape(1, 512), rnorm_g.reshape(1, 512),
      w_out.astype(wdt), caug0, m0, s0)
    x_new, caug, m_f, s_f = outs
    return x_new, caug[..., :DH], caug[..., DH], m_f.reshape(bsz, N_HEADS), s_f


def _odd_proj_kernel(x_ref, sc_ref, sh_ref, ng_ref, w_ref, wvt_ref, cos_ref, sin_ref,
                     pw_ref, ps_ref, past_ref,
                     krow_ref, vrow_ref, kb_ref, q0_ref, q1_ref, vt_ref, yp_ref, pool_ref,
                     r_scr, ext_scr, *, tm, tv, pos0, hi):
    t = pl.program_id(1)
    mm = functools.partial(_mm, hi=hi)

    @pl.when(t == 0)
    def _():
        ext_scr[0:POOL_PAD] = past_ref[0]

    x = x_ref[0]
    h = _rms_mod(x, ng_ref[...], sc_ref[0], sh_ref[0]).astype(_cdt(hi))
    r_scr[...] = mm(h, w_ref[...])
    vt = mm(wvt_ref[...], h, dims=_NT)
    for hd in range(N_HEADS):
        vt_ref[0, hd * VT_ROWS:hd * VT_ROWS + DH, :] = vt[hd * DH:(hd + 1) * DH].astype(vt_ref.dtype)
        vt_ref[0, hd * VT_ROWS + DH:(hd + 1) * VT_ROWS, :] = jnp.ones((VT_ROWS - DH, tm), vt_ref.dtype)
    vrow_ref[0] = r_scr[:, 1536:2048]

    ext_scr[POOL_PAD:POOL_PAD + tm] = r_scr[:, 0:512]
    pos = (pos0 + t * tm + lax.broadcasted_iota(jnp.int32, (tm, 1), 0)).astype(F32)
    for g, w in enumerate(POOL_WINDOWS):
        sl = slice(g * LANES, (g + 1) * LANES)
        pin = r_scr[:, sl]
        win = pin
        for j in range(1, w):
            win = win + ext_scr[POOL_PAD - j:POOL_PAD - j + tm, sl]
        cnt = jnp.minimum(pos + 1.0, float(w))
        dev = win / cnt - pin
        yp_ref[0, :, sl] = (mm(dev, pw_ref[g]) * ps_ref[:, sl]).astype(yp_ref.dtype)
    pool_ref[0] = ext_scr[tv:tv + POOL_PAD]
    ext_scr[0:POOL_PAD] = ext_scr[tm:tm + POOL_PAD]

    lane = lax.broadcasted_iota(jnp.int32, (1, LANES), 1)
    low_half = (lane & (D_DIFF - 1)) < (D_DIFF // 2)
    comp0 = lane < D_DIFF
    cos = cos_ref[...]
    sin = sin_ref[...]

    def rope(a):
        swapped = jnp.where(low_half, pltpu.roll(a, LANES - D_DIFF // 2, 1), pltpu.roll(a, D_DIFF // 2, 1))
        return a * cos + swapped * sin

    for hd in range(N_HEADS):
        sl = slice(hd * LANES, (hd + 1) * LANES)
        q = rope(r_scr[:, 512 + hd * LANES:512 + (hd + 1) * LANES]) * (D_DIFF ** -0.5 * LOG2E)
        q0_ref[0, :, sl] = jnp.where(comp0, q, 0.0).astype(q0_ref.dtype)
        q1_ref[0, :, sl] = jnp.where(comp0, 0.0, q).astype(q1_ref.dtype)
        k = rope(r_scr[:, 1024 + hd * LANES:1024 + (hd + 1) * LANES])
        krow_ref[0, :, sl] = k
        kb_ref[0, :, sl] = k.astype(kb_ref.dtype)


def _odd_proj(x, sc, sh, norm_g, w_in, pool_w, pool_scale, pool_past, pos0, t_valid, tm, hi):
    bsz, t_pad, d = x.shape
    n_t = t_pad // tm
    tv = tm if n_t > 1 else t_valid
    assert n_t == 1 or t_valid == t_pad
    wdt = _cdt(hi)
    w_all = w_in.astype(wdt)
    w_vt = w_in[:, 1536:2048].T.astype(wdt)

    half = D_DIFF // 2
    inv = jnp.power(ROPE_THETA, -jnp.arange(half, dtype=F32) * 2.0 / D_DIFF)
    pos = pos0 + jnp.arange(t_pad, dtype=jnp.int32)
    ang = pos.astype(F32)[:, None] * inv[None, :]
    cos, sin = jnp.cos(ang), jnp.sin(ang)
    cos_t = jnp.concatenate([cos, cos, cos, cos], axis=1)
    sin_t = jnp.concatenate([-sin, sin, -sin, sin], axis=1)
    past = jnp.concatenate([jnp.zeros((bsz, 1, 512), F32), pool_past], axis=1)

    full = lambda shape: _resident(shape)
    per_b = lambda shape: pl.BlockSpec((1,) + shape, lambda b, t: (b,) + (0,) * len(shape))
    tile = lambda n: pl.BlockSpec((1, tm, n), lambda b, t: (b, t, 0))
    f32_rows = jax.ShapeDtypeStruct((bsz, t_pad, 512), F32)
    c_rows = jax.ShapeDtypeStruct((bsz, t_pad, 512), wdt)
    return pl.pallas_call(
        functools.partial(_odd_proj_kernel, tm=tm, tv=tv, pos0=pos0, hi=hi),
        grid=(bsz, n_t),
        in_specs=[
            tile(d), per_b((1, d)), per_b((1, d)), full((1, d)),
            full((d, 2048)), full((512, d)),
            pl.BlockSpec((tm, LANES), lambda b, t: (t, 0)),
            pl.BlockSpec((tm, LANES), lambda b, t: (t, 0)),
            full((4, LANES, LANES)), full((1, 512)), per_b((POOL_PAD, 512)),
        ],
        out_specs=[
            tile(512), tile(512), tile(512), tile(512), tile(512),
            pl.BlockSpec((1, N_HEADS * VT_ROWS, tm), lambda b, t: (b, 0, t)),
            tile(512), per_b((POOL_PAD, 512)),
        ],
        out_shape=[f32_rows, f32_rows, c_rows, c_rows, c_rows,
                   jax.ShapeDtypeStruct((bsz, N_HEADS * VT_ROWS, t_pad), wdt), c_rows,
                   jax.ShapeDtypeStruct((bsz, POOL_PAD, 512), F32)],
        scratch_shapes=[pltpu.VMEM((tm, 2048), F32), pltpu.VMEM((tm + POOL_PAD, 512), F32)],
        compiler_params=_params(("parallel", "arbitrary")),
        name="odd_proj_hi" if hi else "odd_proj",
    )(x, sc, sh, norm_g.reshape(1, d), w_all, w_vt, cos_t, sin_t,
      pool_w.astype(wdt), pool_scale.reshape(1, 512), past)


def _attn_kernel(qi_ref, kj_ref, last_ref,
                 q0_ref, q1_ref, kb_ref, vt_ref, x_ref, yp_ref, g1_ref, lamp_ref, dng_ref, wout_ref,
                 xo_ref, m_scr, acc_scr, *, tq, tk, q_pos0, tk_valid, lam_init, hi):
    s = pl.program_id(1)
    qi = qi_ref[s]
    kj = kj_ref[s]
    mm = functools.partial(_mm, hi=hi)

    @pl.when(kj == 0)
    def _():
        m_scr[...] = jnp.full_like(m_scr, NEG_BIG)
        acc_scr[...] = jnp.zeros_like(acc_scr)

    q_refs = (q0_ref, q1_ref)

    def scores(i):
        sl = slice((i // 2) * LANES, (i // 2 + 1) * LANES)
        return mm(kb_ref[0, :, sl], q_refs[i % 2][0, :, sl], dims=_NT)

    def step(masked):
        if masked:
            kpos = kj * tk + lax.broadcasted_iota(jnp.int32, (tk, tq), 0)
            qpos = q_pos0 + qi * tq + lax.broadcasted_iota(jnp.int32, (tk, tq), 1)
            visible = ((kpos >> MASK_SHIFT) <= (qpos >> MASK_SHIFT)) & (kpos < tk_valid)
        st_next = scores(0)
        for i in range(2 * N_HEADS):
            st = st_next
            if i + 1 < 2 * N_HEADS:
                st_next = scores(i + 1)
            if masked:
                st = jnp.where(visible, st, NEG_BIG)
            hd = i // 2
            vt = vt_ref[0, hd * VT_ROWS:(hd + 1) * VT_ROWS, :]
            m_prev = m_scr[i:i + 1]
            m_new = jnp.maximum(m_prev, jnp.max(st, axis=0, keepdims=True))
            alpha = jnp.exp2(m_prev - m_new)
            p = jnp.exp2(st - m_new)
            acc_scr[i] = alpha * acc_scr[i] + mm(vt, p)
            m_scr[i:i + 1] = m_new

    fully_visible = ((kj + 1) * tk - 1) // MASK_CHUNK <= (q_pos0 + qi * tq) // MASK_CHUNK
    no_pad = (kj + 1) * tk <= tk_valid
    clean = jnp.logical_and(fully_visible, no_pad)
    pl.when(clean)(lambda: step(False))
    pl.when(jnp.logical_not(clean))(lambda: step(True))

    @pl.when(last_ref[s] == 1)
    def _():
        lp = lamp_ref[...]
        lam = (jnp.exp(jnp.sum(lp[0:1] * lp[1:2], axis=1, keepdims=True))
               - jnp.exp(jnp.sum(lp[2:3] * lp[3:4], axis=1, keepdims=True)) + lam_init)
        y = mm(yp_ref[0], wout_ref[0:512, :])
        for hd in range(N_HEADS):
            o0 = acc_scr[2 * hd, 0:DH] / acc_scr[2 * hd, DH:DH + 1]
            o1 = acc_scr[2 * hd + 1, 0:DH] / acc_scr[2 * hd + 1, DH:DH + 1]
            ot = o0 - lam * o1
            ms = jnp.mean(ot * ot, axis=0, keepdims=True)
            ot = (ot * lax.rsqrt(ms + EPS) * dng_ref[...]) * (1.0 - lam_init)
            y = y + mm(ot, wout_ref[512 + hd * LANES:512 + (hd + 1) * LANES, :], dims=_TN)
        xo_ref[0] = x_ref[0] + g1_ref[0] * y


def _attention(q0, q1, kb, vt, x, yp, g1, lam_params, dnorm_g, w_out, q_pos0, tk_valid, lam_init, tq, tk, hi):
    bsz, t_q, d = x.shape
    t_k = kb.shape[1]
    n_q, n_k = t_q // tq, t_k // tk
    qi, kj, last = [], [], []
    for i in range(n_q):
        q_chunk_last = (q_pos0 + (i + 1) * tq - 1) // MASK_CHUNK
        k_last = min((q_chunk_last + 1) * MASK_CHUNK, tk_valid) - 1
        j_last = min(k_last // tk, n_k - 1)
        for j in range(j_last + 1):
            qi.append(i)
            kj.append(j)
            last.append(1 if j == j_last else 0)
    qi, kj, last = (jnp.asarray(a, jnp.int32) for a in (qi, kj, last))
    n_steps = int(qi.shape[0])

    q_spec = pl.BlockSpec((1, tq, 512), lambda b, s, qi, kj, la: (b, qi[s], 0))
    full = lambda shape: _resident(shape)
    grid_spec = pltpu.PrefetchScalarGridSpec(
        num_scalar_prefetch=3,
        grid=(bsz, n_steps),
        in_specs=[
            q_spec, q_spec,
            pl.BlockSpec((1, tk, 512), lambda b, s, qi, kj, la: (b, kj[s], 0)),
            pl.BlockSpec((1, N_HEADS * VT_ROWS, tk), lambda b, s, qi, kj, la: (b, 0, kj[s])),
            pl.BlockSpec((1, tq, d), lambda b, s, qi, kj, la: (b, qi[s], 0)),
            q_spec,
            pl.BlockSpec((1, 1, d), lambda b, s, qi, kj, la: (b, 0, 0)),
            full((4, D_DIFF)), full((LANES, 1)), full((d, d)),
        ],
        out_specs=pl.BlockSpec((1, tq, d), lambda b, s, qi, kj, la: (b, qi[s], 0)),
        scratch_shapes=[
            pltpu.VMEM((2 * N_HEADS, tq), F32),
            pltpu.VMEM((2 * N_HEADS, VT_ROWS, tq), F32),
        ],
    )
    return pl.pallas_call(
        functools.partial(_attn_kernel, tq=tq, tk=tk, q_pos0=q_pos0, tk_valid=tk_valid,
                          lam_init=lam_init, hi=hi),
        grid_spec=grid_spec,
        out_shape=jax.ShapeDtypeStruct((bsz, t_q, d), F32),
        compiler_params=_params(("parallel", "arbitrary")),
        name="diff_attention_hi" if hi else "diff_attention",
    )(qi, kj, last, q0, q1, kb, vt, x, yp, g1, lam_params, dnorm_g.reshape(LANES, 1),
      w_out.astype(_cdt(hi)))


def _route(h, rw, rb):
    s = jax.nn.sigmoid(_mm(h, rw, True))
    sel = s + rb
    lane_i = lax.broadcasted_iota(jnp.int32, sel.shape, 1)
    lane = lane_i.astype(F32)
    group = (lane_i >> 2).astype(F32)

    def top2(vals):
        m1 = jnp.max(vals, axis=1, keepdims=True)
        i1 = jnp.min(jnp.where(vals == m1, lane, float(N_EXPERTS)), axis=1, keepdims=True)
        rest = jnp.where(lane == i1, -jnp.inf, vals)
        m2 = jnp.max(rest, axis=1, keepdims=True)
        i2 = jnp.min(jnp.where(rest == m2, lane, float(N_EXPERTS)), axis=1, keepdims=True)
        return m1, i1, m2, i2

    best = None
    for g in range(N_EXPERTS // EPG):
        m1, _, m2, _ = top2(jnp.where(group == float(g), sel, -jnp.inf))
        score = m1 + m2
        if best is None:
            best, gbest = score, jnp.zeros_like(score)
        else:
            better = score > best
            best = jnp.where(better, score, best)
            gbest = jnp.where(better, float(g), gbest)
    _, i1, _, i2 = top2(jnp.where(group == gbest, sel, -jnp.inf))
    w1 = jnp.sum(jnp.where(lane == i1, s, 0.0), axis=1, keepdims=True)
    w2 = jnp.sum(jnp.where(lane == i2, s, 0.0), axis=1, keepdims=True)
    tot = w1 + w2
    return jnp.where(lane == i1, w1 / tot, 0.0) + jnp.where(lane == i2, w2 / tot, 0.0)


def _moe_kernel(x_ref, sc_ref, sh_ref, g2_ref, ng_ref, rw_ref, rb_ref, w1_ref, w3_ref, w2_ref, fg_ref,
                o_ref, h_scr, gate_scr, acc_scr, *, final_norm, hi):
    e = pl.program_id(1)
    mm = functools.partial(_mm, hi=hi)

    @pl.when(e == 0)
    def _():
        h = _rms_mod(x_ref[0], ng_ref[...], sc_ref[0], sh_ref[0])
        h_scr[...] = h.astype(h_scr.dtype)
        gate_scr[...] = _route(h, rw_ref[...], rb_ref[...])
        acc_scr[...] = jnp.zeros_like(acc_scr)

    hb = h_scr[...]
    a = mm(hb, w1_ref[0])
    b = mm(hb, w3_ref[0])
    ff = mm(a * jax.nn.sigmoid(a) * b, w2_ref[0])
    lane = lax.broadcasted_iota(jnp.int32, gate_scr.shape, 1)
    gate = jnp.sum(jnp.where(lane == e, gate_scr[...], 0.0), axis=1, keepdims=True)
    acc_scr[...] += gate * ff

    @pl.when(e == N_EXPERTS - 1)
    def _():
        y = x_ref[0] + g2_ref[0] * acc_scr[...]
        if final_norm:
            ms = jnp.mean(y * y, axis=-1, keepdims=True)
            y = y * lax.rsqrt(ms + EPS) * fg_ref[...]
        o_ref[0] = y


def _moe(x, sc, sh, g2, norm_g, router_w, router_b, w1, w3, w2, final_g, final_norm, tm, hi):
    bsz, t_len, d = x.shape
    n_t = t_len // tm
    rows = sc.shape[1]
    if rows == 1:
        mod_spec = pl.BlockSpec((1, 1, d), lambda i, e: (i // n_t, 0, 0))
    else:
        mod_spec = pl.BlockSpec((1, tm, d), lambda i, e: (i // n_t, i % n_t, 0))
    x_spec = pl.BlockSpec((1, tm, d), lambda i, e: (i // n_t, i % n_t, 0))
    full = lambda shape: _resident(shape)
    dff = w1.shape[-1]
    return pl.pallas_call(
        functools.partial(_moe_kernel, final_norm=final_norm, hi=hi),
        grid=(bsz * n_t, N_EXPERTS),
        in_specs=[
            x_spec, mod_spec, mod_spec, mod_spec, full((1, d)),
            full((d, N_EXPERTS)), full((1, N_EXPERTS)),
            pl.BlockSpec((1, d, dff), lambda i, e: (e, 0, 0)),
            pl.BlockSpec((1, d, dff), lambda i, e: (e, 0, 0)),
            pl.BlockSpec((1, dff, d), lambda i, e: (e, 0, 0)),
            full((1, d)),
        ],
        out_specs=x_spec,
        out_shape=jax.ShapeDtypeStruct((bsz, t_len, d), F32),
        scratch_shapes=[
            pltpu.VMEM((tm, d), _cdt(hi)),
            pltpu.VMEM((tm, N_EXPERTS), F32),
            pltpu.VMEM((tm, d), F32),
        ],
        compiler_params=_params(("parallel", "arbitrary")),
        name="moe_hi" if hi else "moe",
    )(x, sc, sh, g2, norm_g.reshape(1, d), router_w, router_b.reshape(1, N_EXPERTS), w1, w3, w2,
      final_g.reshape(1, d))


def _trunk(x, mods, pos0, t_valid, states, params, moe_weights, flat_moe, attn_tiles, proj_tm, moe_tm, hi):
    (norm_mix_g, norm_ffn_g, final_norm_g, w_in_even, i_bias, f_bias, mnorm_g, rnorm_g, w_out_even,
     w_in_odd, pool_w, pool_scale, lq1, lk1, lq2, lk2, dnorm_g, w_out_odd, router_w, router_b) = params
    c0, n0, m0, s0, pool_past, k_past, v_past = states
    bsz, t_pad, d = x.shape
    w1, w3, w2 = moe_weights

    def run_moe(x, layer, final):
        sc2, sh2, g2 = mods[layer][4], mods[layer][3], mods[layer][5]
        if flat_moe:
            n_rows = bsz * t_valid
            flat = lambda a: jnp.broadcast_to(a, (bsz, t_valid, d)).reshape(1, n_rows, d)
            y = _moe(x[:, :t_valid].reshape(1, n_rows, d), flat(sc2), flat(sh2), flat(g2), norm_ffn_g[layer],
                     router_w, router_b, w1[layer], w3[layer], w2[layer], final_norm_g, final, n_rows, hi)
            y = y.reshape(bsz, t_valid, d)
            return jnp.concatenate([y, jnp.zeros((bsz, t_pad - t_valid, d), F32)], axis=1)
        return _moe(x, sc2, sh2, g2, norm_ffn_g[layer], router_w, router_b,
                    w1[layer], w3[layer], w2[layer], final_norm_g, final, moe_tm, hi)

    sh1, sc1, g1 = mods[0][0], mods[0][1], mods[0][2]
    x, c_f, n_f, m_f, s_f = _even_mixer(x, sc1, sh1, g1, norm_mix_g[0], w_in_even[0], i_bias[0], f_bias[0],
                                        mnorm_g[0], rnorm_g[0], w_out_even[0], c0, n0, m0, s0, pos0, t_valid, hi)
    x = run_moe(x, 0, False)

    layer = 1
    lam_init = 0.8 - 0.6 * math.exp(-0.3 * layer)
    sh1, sc1, g1 = mods[1][0], mods[1][1], mods[1][2]
    k_rows, v_rows, kb, q0, q1, vt, yp, pool_new = _odd_proj(
        x, sc1, sh1, norm_mix_g[1], w_in_odd[0], pool_w[0], pool_scale[0], pool_past, pos0, t_valid, proj_tm, hi)
    if k_past is None:
        tk_valid = t_valid
    else:
        past_len = k_past.shape[1]
        tk_valid = past_len + t_valid
        kb = jnp.concatenate([k_past.reshape(bsz, past_len, 512).astype(kb.dtype), kb], axis=1)
        vt_past = jnp.concatenate([jnp.transpose(v_past, (0, 2, 3, 1)),
                                   jnp.ones((bsz, N_HEADS, VT_ROWS - DH, past_len), F32)], axis=2)
        vt = jnp.concatenate([vt_past.reshape(bsz, N_HEADS * VT_ROWS, past_len).astype(vt.dtype), vt], axis=2)
    lam_params = jnp.stack([lq1[0], lk1[0], lq2[0], lk2[0]])
    tq, tk = attn_tiles
    x = _attention(q0, q1, kb, vt, x, yp, g1, lam_params, dnorm_g[0], w_out_odd[0],
                   pos0, tk_valid, lam_init, tq, tk, hi)
    x = run_moe(x, 1, True)
    return x, (c_f, n_f, m_f, s_f), (pool_new[:, 1:], k_rows[:, :t_valid], v_rows[:, :t_valid])


def kernel(x_prompt, x_sample, c_prompt, c_sample, state_mlstm_c, state_mlstm_n, state_mlstm_m, state_ret, state_pool, cache_k, cache_v, ada_w, ada_b, norm_mix_g, norm_ffn_g, final_norm_g, w_in_even, mlstm_i_bias, mlstm_f_bias, mlstm_norm_g, ret_norm_g, w_out_even, w_in_odd, pool_w, pool_scale, lambda_q1, lambda_k1, lambda_q2, lambda_k2, diff_norm_g, w_out_odd, router_w, router_b, moe_w1, moe_w3, moe_w2):
    bp, tp, d = x_prompt.shape
    bs, ts, _ = x_sample.shape
    past_len = cache_k.shape[2]
    n_layers = ada_w.shape[0]

    c_all = jnp.concatenate([c_prompt, c_sample, jnp.zeros((16 - bp - bs, d), F32)], axis=0)
    mod_all = _adaln(c_all, ada_w, ada_b)

    def mods_for(lo, stop):
        return [[mod_all[l, lo:stop, None, k * d:(k + 1) * d] for k in range(6)] for l in range(n_layers)]

    params = (norm_mix_g, norm_ffn_g, final_norm_g, w_in_even, mlstm_i_bias, mlstm_f_bias,
              mlstm_norm_g.reshape(-1, 512), ret_norm_g.reshape(-1, 512), w_out_even,
              w_in_odd, pool_w, pool_scale, lambda_q1, lambda_k1, lambda_q2, lambda_k2, diff_norm_g,
              w_out_odd, router_w, router_b)

    zeros = lambda *s: jnp.zeros(s, F32)
    p_states = (zeros(bp, N_HEADS, DH, DH), zeros(bp, N_HEADS, DH), zeros(bp, N_HEADS),
                zeros(bp, N_HEADS, DH, DH), zeros(bp, POOL_PAD - 1, 512), None, None)
    moe_bf16 = (moe_w1.astype(BF16), moe_w3.astype(BF16), moe_w2.astype(BF16))
    y_p, ev_p, od_p = _trunk(x_prompt, mods_for(0, bp), 0, tp, p_states, params, moe_bf16,
                             flat_moe=False, attn_tiles=(min(1024, tp), min(512, tp)), proj_tm=min(512, tp),
                             moe_tm=min(1024, tp), hi=False)

    ts_pad = LANES
    x_s = jnp.concatenate([x_sample, zeros(bs, ts_pad - ts, d)], axis=1)
    s_states = (state_mlstm_c[0], state_mlstm_n[0], state_mlstm_m[0], state_ret[0], state_pool[0],
                cache_k[0], cache_v[0])
    y_s, ev_s, od_s = _trunk(x_s, mods_for(bp, bp + bs), past_len, ts, s_states, params,
                             (moe_w1, moe_w3, moe_w2), flat_moe=True,
                             attn_tiles=(ts_pad, past_len + ts_pad), proj_tm=ts_pad, moe_tm=None, hi=True)
    y_s = y_s[:, :ts]

    hd4 = lambda a: a.reshape(a.shape[0], a.shape[1], N_HEADS, DH)
    return (y_p, y_s,
            ev_p[0][None], ev_p[1][None], ev_p[2][None], ev_p[3][None],
            od_p[0][None], hd4(od_p[1])[None], hd4(od_p[2])[None],
            ev_s[0][None], ev_s[1][None], ev_s[2][None], ev_s[3][None],
            od_s[0][None], hd4(od_s[1])[None], hd4(od_s[2])[None])
```

```python
import functools
import math

import jax
import jax.numpy as jnp
from jax import lax
from jax.experimental import pallas as pl
from jax.experimental.pallas import tpu as pltpu

F32 = jnp.float32
BF16 = jnp.bfloat16

EPS = 1e-6
ROPE_THETA = 10000.0
MASK_CHUNK = 64
MASK_SHIFT = 6
DH = 128
N_HEADS = 4
D_DIFF = 64
POOL_WINDOWS = (2, 4, 8, 16)
POOL_PAD = 16
N_EXPERTS = 16
EPG = 4
EVEN_CHUNKS_PER_STEP = 4
EXPERTS_PER_STEP = 2
STEPS_PER_GROUP = EPG // EXPERTS_PER_STEP
NEG_BIG = -1e30
LANES = 128
VT_ROWS = DH + 16
LOG2E = 1.4426950408889634
VMEM_LIMIT = 52 * 1024 * 1024

_NT = (((1,), (1,)), ((), ()))
_TN = (((0,), (0,)), ((), ()))


def _mm(a, b, hi, dims=None):
    if hi:
        a, b, prec = a.astype(F32), b.astype(F32), lax.Precision.HIGHEST
    else:
        a, b, prec = a.astype(BF16), b.astype(BF16), None
    if dims is None:
        return jnp.dot(a, b, precision=prec, preferred_element_type=F32)
    return lax.dot_general(a, b, dims, precision=prec, preferred_element_type=F32)


def _cdt(hi):
    return F32 if hi else BF16


def _rms_mod(x, g, sc, sh):
    ms = jnp.mean(x * x, axis=-1, keepdims=True)
    return (x * lax.rsqrt(ms + EPS) * g) * (1.0 + sc) + sh


def _params(sem):
    return pltpu.CompilerParams(dimension_semantics=sem, vmem_limit_bytes=VMEM_LIMIT)


def _resident(shape):
    zeros = (0,) * len(shape)
    return pl.BlockSpec(shape, lambda *args: zeros, pipeline_mode=pl.Buffered(1))


def _adaln_kernel(c_ref, w_ref, b_ref, o_ref):
    c = c_ref[...]
    o_ref[0] = _mm(c * jax.nn.sigmoid(c), w_ref[0], True) + b_ref[0]


def _adaln(c_all, ada_w, ada_b):
    n_layers, d, n6 = ada_w.shape
    rows = c_all.shape[0]
    bn = 1536
    return pl.pallas_call(
        _adaln_kernel,
        grid=(n_layers, n6 // bn),
        in_specs=[
            pl.BlockSpec((rows, d), lambda l, j: (0, 0)),
            pl.BlockSpec((1, d, bn), lambda l, j: (l, 0, j)),
            pl.BlockSpec((1, 1, bn), lambda l, j: (l, 0, j)),
        ],
        out_specs=pl.BlockSpec((1, rows, bn), lambda l, j: (l, 0, j)),
        out_shape=jax.ShapeDtypeStruct((n_layers, rows, n6), F32),
        compiler_params=_params(("parallel", "parallel")),
        name="adaln",
    )(c_all, ada_w, ada_b.reshape(n_layers, 1, n6))


def _log_sigmoid(x):
    return jnp.minimum(x, 0.0) - jnp.log1p(jnp.exp(-jnp.abs(x)))


def _even_kernel(x_ref, sc_ref, sh_ref, g1_ref, ng_ref, wrow_ref, wkt_ref, wgt_ref, gb_ref,
                 cosr_ref, sinr_ref, cost_ref, sint_ref, mng_ref, rng_ref, wout_ref,
                 c0_ref, m0_ref, s0_ref,
                 xo_ref, cf_ref, mf_ref, sf_ref,
                 r_scr, kt_scr, c_scr, m_scr, s_scr, y_scr, *, lc, n_c, lv, hi):
    t = pl.program_id(1)
    n_t = pl.num_programs(1)
    mm = functools.partial(_mm, hi=hi)

    @pl.when(t == 0)
    def _():
        c_scr[...] = c0_ref[0]
        m_scr[...] = m0_ref[0]
        s_scr[...] = s0_ref[0]

    x = x_ref[0]
    h = _rms_mod(x, ng_ref[...], sc_ref[0], sh_ref[0]).astype(_cdt(hi))
    r_scr[...] = mm(h, wrow_ref[...])
    kt_scr[...] = mm(wkt_ref[...], h, dims=_NT) * (DH ** -0.5)
    gt = mm(wgt_ref[...], h, dims=_NT) + gb_ref[...]

    lane = lax.broadcasted_iota(jnp.int32, (1, lc), 1)
    valid = lane < lv
    row_i = lax.broadcasted_iota(jnp.int32, (lc, lc), 0)
    col_i = lax.broadcasted_iota(jnp.int32, (lc, lc), 1)
    tri = col_i <= row_i
    eye = col_i == row_i
    upper = (row_i <= col_i).astype(F32)
    rel = (row_i - col_i).astype(F32)
    jrow = lane.astype(F32)
    jcol = lax.broadcasted_iota(jnp.int32, (lc, 1), 0).astype(F32)
    ones_col = (lax.broadcasted_iota(jnp.int32, (lc, DH), 1) == 0).astype(_cdt(hi))

    for c in range(n_c):
        rs = slice(c * lc, (c + 1) * lc)
        ig = gt[0:N_HEADS, rs]
        lf = jnp.where(valid, _log_sigmoid(gt[N_HEADS:2 * N_HEADS, rs]), 0.0)
        b_all = _mm(lf, upper, True)
        u_all = jnp.where(valid, ig - b_all, -jnp.inf)

        for hd in range(N_HEADS):
            sl = slice(hd * DH, (hd + 1) * DH)
            q = r_scr[rs, sl].astype(_cdt(hi))
            v = r_scr[rs, 512 + hd * DH:512 + (hd + 1) * DH].astype(_cdt(hi))
            vaug = jnp.concatenate([v, ones_col], axis=1)
            kt = kt_scr[sl, rs]
            u_row = u_all[hd:hd + 1]
            b_row = b_all[hd:hd + 1]
            m_prev = m_scr[hd]
            caug = c_scr[hd]

            u_mat = jnp.where(tri, u_row, -jnp.inf)
            r_col = jnp.maximum(m_prev, jnp.max(u_mat, axis=1, keepdims=True))
            b_col = jnp.sum(jnp.where(eye, b_row, 0.0), axis=1, keepdims=True)
            s_qk = mm(q, kt)
            p = jnp.exp(u_mat - r_col) * s_qk
            sc_col = jnp.exp(m_prev - r_col)
            nd = sc_col * mm(q, caug) + mm(p, vaug)
            num = nd[:, :DH]
            den = nd[:, DH:DH + 1]
            m_t = b_col + r_col
            h_m = num / jnp.maximum(jnp.abs(den), jnp.exp(-m_t))
            ms = jnp.mean(h_m * h_m, axis=-1, keepdims=True)
            h_m = h_m * lax.rsqrt(ms + EPS) * mng_ref[:, sl]
            gate = jax.nn.sigmoid(r_scr[rs, 1024 + hd * DH:1024 + (hd + 1) * DH])
            y_scr[rs, sl] = (gate * h_m).astype(y_scr.dtype)

            u_max = jnp.max(u_row, axis=1, keepdims=True)
            w_row = jnp.exp(u_row - u_max)
            kv = mm(kt * w_row, vaug)
            b_end = b_row[:, lc - 1:lc]
            a_max = b_end + u_max
            m_new = jnp.maximum(b_end + m_prev, a_max)
            c_scr[hd] = jnp.exp(b_end + m_prev - m_new) * caug + jnp.exp(a_max - m_new) * kv
            m_scr[hd] = m_new

        cos_r = cosr_ref[rs, :]
        sin_r = sinr_ref[rs, :]
        cos_t = cost_ref[:, rs]
        sin_t = sint_ref[:, rs]
        for hd in range(N_HEADS):
            lg = math.log(1.0 - 2.0 ** (-5.0 - hd))
            sl = slice(hd * DH, (hd + 1) * DH)
            rq = r_scr[rs, 1536 + hd * DH:1536 + (hd + 1) * DH]
            rq = (rq * cos_r + pltpu.roll(rq, DH // 2, 1) * sin_r).astype(_cdt(hi))
            rv = r_scr[rs, 2048 + hd * DH:2048 + (hd + 1) * DH].astype(_cdt(hi))
            kt = kt_scr[512 + hd * DH:512 + (hd + 1) * DH, rs]
            top, bot = kt[:DH // 2], kt[DH // 2:]
            kt = jnp.concatenate([top * cos_t - bot * sin_t, top * sin_t + bot * cos_t], axis=0)
            s_prev = s_scr[hd]

            decay = jnp.where(rel >= 0, jnp.exp(lg * jnp.maximum(rel, 0.0)), 0.0)
            scores = mm(rq, kt) * decay
            intra = mm(scores, rv)
            inter = mm(rq, s_prev) * jnp.exp(lg * (jcol + 1.0))
            o = intra + inter
            ms = jnp.mean(o * o, axis=-1, keepdims=True)
            o = o * lax.rsqrt(ms + EPS) * rng_ref[:, sl]
            g_in = r_scr[rs, 2560 + hd * DH:2560 + (hd + 1) * DH]
            y_scr[rs, 512 + hd * DH:512 + (hd + 1) * DH] = (g_in * jax.nn.sigmoid(g_in) * o).astype(y_scr.dtype)

            w_end = jnp.where(valid, jnp.exp(lg * (lv - 1.0 - jrow)), 0.0)
            s_scr[hd] = math.exp(lg * lv) * s_prev + mm(kt * w_end, rv)

    xo_ref[0] = x + g1_ref[0] * mm(y_scr[...], wout_ref[...])

    @pl.when(t == n_t - 1)
    def _():
        cf_ref[0] = c_scr[...]
        mf_ref[0] = m_scr[...]
        sf_ref[0] = s_scr[...]


def _even_mixer(x, sc, sh, g1, norm_g, w_in, i_bias, f_bias, mnorm_g, rnorm_g, w_out,
                c0, n0, m0, s0, pos0, t_valid, hi):
    bsz, t_pad, d = x.shape
    lc = LANES
    n_c = max(1, min(EVEN_CHUNKS_PER_STEP, t_pad // lc))
    tm = n_c * lc
    n_t = t_pad // tm
    lv = lc if t_pad > lc else t_valid
    assert t_pad == lc or t_valid == t_pad
    wdt = _cdt(hi)
    w = w_in
    cols = lambda a, b: w[:, a:b]
    w_row = jnp.concatenate([cols(0, 512), cols(1024, 1536), cols(1536, 2048),
                             cols(2056, 2568), cols(3080, 3592), cols(3592, 4104)], axis=1).astype(wdt)
    w_kt = jnp.concatenate([cols(512, 1024), cols(2568, 3080)], axis=1).T.astype(wdt)
    w_gt = jnp.concatenate([cols(2048, 2056).T, jnp.zeros((8, d), F32)], axis=0).astype(wdt)
    gb = jnp.concatenate([i_bias, f_bias, jnp.zeros((8,), F32)]).reshape(16, 1)

    half = DH // 2
    inv = jnp.power(ROPE_THETA, -jnp.arange(half, dtype=F32) * 2.0 / DH)
    pos = pos0 + jnp.arange(t_pad, dtype=jnp.int32)
    ang = pos.astype(F32)[:, None] * inv[None, :]
    cos, sin = jnp.cos(ang), jnp.sin(ang)
    cos_r = jnp.concatenate([cos, cos], axis=1)
    sin_r = jnp.concatenate([-sin, sin], axis=1)
    cos_t, sin_t = cos.T, sin.T

    caug0 = jnp.concatenate([c0, n0[..., None], jnp.zeros(c0.shape[:-1] + (DH - 1,), F32)], axis=-1)
    m0 = m0.reshape(bsz, N_HEADS, 1, 1)

    full = lambda shape: _resident(shape)
    per_b = lambda shape: pl.BlockSpec((1,) + shape, lambda b, t: (b,) + (0,) * len(shape))
    outs = pl.pallas_call(
        functools.partial(_even_kernel, lc=lc, n_c=n_c, lv=lv, hi=hi),
        grid=(bsz, n_t),
        in_specs=[
            pl.BlockSpec((1, tm, d), lambda b, t: (b, t, 0)),
            per_b((1, d)), per_b((1, d)), per_b((1, d)),
            full((1, d)),
            full((d, 3072)), full((1024, d)), full((16, d)), full((16, 1)),
            pl.BlockSpec((tm, DH), lambda b, t: (t, 0)),
            pl.BlockSpec((tm, DH), lambda b, t: (t, 0)),
            pl.BlockSpec((half, tm), lambda b, t: (0, t)),
            pl.BlockSpec((half, tm), lambda b, t: (0, t)),
            full((1, 512)), full((1, 512)), full((d, d)),
            per_b((N_HEADS, DH, 2 * DH)), per_b((N_HEADS, 1, 1)), per_b((N_HEADS, DH, DH)),
        ],
        out_specs=[
            pl.BlockSpec((1, tm, d), lambda b, t: (b, t, 0)),
            per_b((N_HEADS, DH, 2 * DH)), per_b((N_HEADS, 1, 1)), per_b((N_HEADS, DH, DH)),
        ],
        out_shape=[
            jax.ShapeDtypeStruct((bsz, t_pad, d), F32),
            jax.ShapeDtypeStruct((bsz, N_HEADS, DH, 2 * DH), F32),
            jax.ShapeDtypeStruct((bsz, N_HEADS, 1, 1), F32),
            jax.ShapeDtypeStruct((bsz, N_HEADS, DH, DH), F32),
        ],
        scratch_shapes=[
            pltpu.VMEM((tm, 3072), F32),
            pltpu.VMEM((1024, tm), F32),
            pltpu.VMEM((N_HEADS, DH, 2 * DH), F32),
            pltpu.VMEM((N_HEADS, 1, 1), F32),
            pltpu.VMEM((N_HEADS, DH, DH), F32),
            pltpu.VMEM((tm, d), wdt),
        ],
        compiler_params=_params(("parallel", "arbitrary")),
        name="even_mixer_hi" if hi else "even_mixer",
    )(x, sc, sh, g1, norm_g.reshape(1, d), w_row, w_kt, w_gt, gb,
      cos_r, sin_r, cos_t, sin_t, mnorm_g.reshape(1, 512), rnorm_g.reshape(1, 512),
      w_out.astype(wdt), caug0, m0, s0)
    x_new, caug, m_f, s_f = outs
    return x_new, caug[..., :DH], caug[..., DH], m_f.reshape(bsz, N_HEADS), s_f


def _odd_proj_kernel(x_ref, sc_ref, sh_ref, ng_ref, w_ref, wvt_ref, cos_ref, sin_ref,
                     pw_ref, ps_ref, past_ref,
                     krow_ref, vrow_ref, kb_ref, q0_ref, q1_ref, vt_ref, yp_ref, pool_ref,
                     r_scr, ext_scr, *, tm, tv, pos0, hi):
    t = pl.program_id(1)
    mm = functools.partial(_mm, hi=hi)

    @pl.when(t == 0)
    def _():
        ext_scr[0:POOL_PAD] = past_ref[0]

    x = x_ref[0]
    h = _rms_mod(x, ng_ref[...], sc_ref[0], sh_ref[0]).astype(_cdt(hi))
    r_scr[...] = mm(h, w_ref[...])
    vt = mm(wvt_ref[...], h, dims=_NT)
    for hd in range(N_HEADS):
        vt_ref[0, hd * VT_ROWS:hd * VT_ROWS + DH, :] = vt[hd * DH:(hd + 1) * DH].astype(vt_ref.dtype)
        vt_ref[0, hd * VT_ROWS + DH:(hd + 1) * VT_ROWS, :] = jnp.ones((VT_ROWS - DH, tm), vt_ref.dtype)
    vrow_ref[0] = r_scr[:, 1536:2048]

    ext_scr[POOL_PAD:POOL_PAD + tm] = r_scr[:, 0:512]
    pos = (pos0 + t * tm + lax.broadcasted_iota(jnp.int32, (tm, 1), 0)).astype(F32)
    for g, w in enumerate(POOL_WINDOWS):
        sl = slice(g * LANES, (g + 1) * LANES)
        pin = r_scr[:, sl]
        win = pin
        for j in range(1, w):
            win = win + ext_scr[POOL_PAD - j:POOL_PAD - j + tm, sl]
        cnt = jnp.minimum(pos + 1.0, float(w))
        dev = win / cnt - pin
        yp_ref[0, :, sl] = (mm(dev, pw_ref[g]) * ps_ref[:, sl]).astype(yp_ref.dtype)
    pool_ref[0] = ext_scr[tv:tv + POOL_PAD]
    ext_scr[0:POOL_PAD] = ext_scr[tm:tm + POOL_PAD]

    lane = lax.broadcasted_iota(jnp.int32, (1, LANES), 1)
    low_half = (lane & (D_DIFF - 1)) < (D_DIFF // 2)
    comp0 = lane < D_DIFF
    cos = cos_ref[...]
    sin = sin_ref[...]

    def rope(a):
        swapped = jnp.where(low_half, pltpu.roll(a, LANES - D_DIFF // 2, 1), pltpu.roll(a, D_DIFF // 2, 1))
        return a * cos + swapped * sin

    for hd in range(N_HEADS):
        sl = slice(hd * LANES, (hd + 1) * LANES)
        q = rope(r_scr[:, 512 + hd * LANES:512 + (hd + 1) * LANES]) * (D_DIFF ** -0.5 * LOG2E)
        q0_ref[0, :, sl] = jnp.where(comp0, q, 0.0).astype(q0_ref.dtype)
        q1_ref[0, :, sl] = jnp.where(comp0, 0.0, q).astype(q1_ref.dtype)
        k = rope(r_scr[:, 1024 + hd * LANES:1024 + (hd + 1) * LANES])
        krow_ref[0, :, sl] = k
        kb_ref[0, :, sl] = k.astype(kb_ref.dtype)


def _odd_proj(x, sc, sh, norm_g, w_in, pool_w, pool_scale, pool_past, pos0, t_valid, tm, hi):
    bsz, t_pad, d = x.shape
    n_t = t_pad // tm
    tv = tm if n_t > 1 else t_valid
    assert n_t == 1 or t_valid == t_pad
    wdt = _cdt(hi)
    w_all = w_in.astype(wdt)
    w_vt = w_in[:, 1536:2048].T.astype(wdt)

    half = D_DIFF // 2
    inv = jnp.power(ROPE_THETA, -jnp.arange(half, dtype=F32) * 2.0 / D_DIFF)
    pos = pos0 + jnp.arange(t_pad, dtype=jnp.int32)
    ang = pos.astype(F32)[:, None] * inv[None, :]
    cos, sin = jnp.cos(ang), jnp.sin(ang)
    cos_t = jnp.concatenate([cos, cos, cos, cos], axis=1)
    sin_t = jnp.concatenate([-sin, sin, -sin, sin], axis=1)
    past = jnp.concatenate([jnp.zeros((bsz, 1, 512), F32), pool_past], axis=1)

    full = lambda shape: _resident(shape)
    per_b = lambda shape: pl.BlockSpec((1,) + shape, lambda b, t: (b,) + (0,) * len(shape))
    tile = lambda n: pl.BlockSpec((1, tm, n), lambda b, t: (b, t, 0))
    f32_rows = jax.ShapeDtypeStruct((bsz, t_pad, 512), F32)
    c_rows = jax.ShapeDtypeStruct((bsz, t_pad, 512), wdt)
    return pl.pallas_call(
        functools.partial(_odd_proj_kernel, tm=tm, tv=tv, pos0=pos0, hi=hi),
        grid=(bsz, n_t),
        in_specs=[
            tile(d), per_b((1, d)), per_b((1, d)), full((1, d)),
            full((d, 2048)), full((512, d)),
            pl.BlockSpec((tm, LANES), lambda b, t: (t, 0)),
            pl.BlockSpec((tm, LANES), lambda b, t: (t, 0)),
            full((4, LANES, LANES)), full((1, 512)), per_b((POOL_PAD, 512)),
        ],
        out_specs=[
            tile(512), tile(512), tile(512), tile(512), tile(512),
            pl.BlockSpec((1, N_HEADS * VT_ROWS, tm), lambda b, t: (b, 0, t)),
            tile(512), per_b((POOL_PAD, 512)),
        ],
        out_shape=[f32_rows, f32_rows, c_rows, c_rows, c_rows,
                   jax.ShapeDtypeStruct((bsz, N_HEADS * VT_ROWS, t_pad), wdt), c_rows,
                   jax.ShapeDtypeStruct((bsz, POOL_PAD, 512), F32)],
        scratch_shapes=[pltpu.VMEM((tm, 2048), F32), pltpu.VMEM((tm + POOL_PAD, 512), F32)],
        compiler_params=_params(("parallel", "arbitrary")),
        name="odd_proj_hi" if hi else "odd_proj",
    )(x, sc, sh, norm_g.reshape(1, d), w_all, w_vt, cos_t, sin_t,
      pool_w.astype(wdt), pool_scale.reshape(1, 512), past)


def _attn_kernel(qi_ref, kj_ref, last_ref,
                 q0_ref, q1_ref, kb_ref, vt_ref, x_ref, yp_ref, g1_ref, lamp_ref, dng_ref, wout_ref,
                 xo_ref, m_scr, acc_scr, *, tq, tk, q_pos0, tk_valid, lam_init, hi):
    s = pl.program_id(1)
    qi = qi_ref[s]
    kj = kj_ref[s]
    mm = functools.partial(_mm, hi=hi)

    @pl.when(kj == 0)
    def _():
        m_scr[...] = jnp.full_like(m_scr, NEG_BIG)
        acc_scr[...] = jnp.zeros_like(acc_scr)

    q_refs = (q0_ref, q1_ref)

    def scores(i):
        sl = slice((i // 2) * LANES, (i // 2 + 1) * LANES)
        return mm(kb_ref[0, :, sl], q_refs[i % 2][0, :, sl], dims=_NT)

    def step(masked):
        if masked:
            kpos = kj * tk + lax.broadcasted_iota(jnp.int32, (tk, tq), 0)
            qpos = q_pos0 + qi * tq + lax.broadcasted_iota(jnp.int32, (tk, tq), 1)
            visible = ((kpos >> MASK_SHIFT) <= (qpos >> MASK_SHIFT)) & (kpos < tk_valid)
        st_next = scores(0)
        for i in range(2 * N_HEADS):
            st = st_next
            if i + 1 < 2 * N_HEADS:
                st_next = scores(i + 1)
            if masked:
                st = jnp.where(visible, st, NEG_BIG)
            hd = i // 2
            vt = vt_ref[0, hd * VT_ROWS:(hd + 1) * VT_ROWS, :]
            m_prev = m_scr[i:i + 1]
            m_new = jnp.maximum(m_prev, jnp.max(st, axis=0, keepdims=True))
            alpha = jnp.exp2(m_prev - m_new)
            p = jnp.exp2(st - m_new)
            acc_scr[i] = alpha * acc_scr[i] + mm(vt, p)
            m_scr[i:i + 1] = m_new

    fully_visible = ((kj + 1) * tk - 1) // MASK_CHUNK <= (q_pos0 + qi * tq) // MASK_CHUNK
    no_pad = (kj + 1) * tk <= tk_valid
    clean = jnp.logical_and(fully_visible, no_pad)
    pl.when(clean)(lambda: step(False))
    pl.when(jnp.logical_not(clean))(lambda: step(True))

    @pl.when(last_ref[s] == 1)
    def _():
        lp = lamp_ref[...]
        lam = (jnp.exp(jnp.sum(lp[0:1] * lp[1:2], axis=1, keepdims=True))
               - jnp.exp(jnp.sum(lp[2:3] * lp[3:4], axis=1, keepdims=True)) + lam_init)
        y = mm(yp_ref[0], wout_ref[0:512, :])
        for hd in range(N_HEADS):
            o0 = acc_scr[2 * hd, 0:DH] / acc_scr[2 * hd, DH:DH + 1]
            o1 = acc_scr[2 * hd + 1, 0:DH] / acc_scr[2 * hd + 1, DH:DH + 1]
            ot = o0 - lam * o1
            ms = jnp.mean(ot * ot, axis=0, keepdims=True)
            ot = (ot * lax.rsqrt(ms + EPS) * dng_ref[...]) * (1.0 - lam_init)
            y = y + mm(ot, wout_ref[512 + hd * LANES:512 + (hd + 1) * LANES, :], dims=_TN)
        xo_ref[0] = x_ref[0] + g1_ref[0] * y


def _attention(q0, q1, kb, vt, x, yp, g1, lam_params, dnorm_g, w_out, q_pos0, tk_valid, lam_init, tq, tk, hi):
    bsz, t_q, d = x.shape
    t_k = kb.shape[1]
    n_q, n_k = t_q // tq, t_k // tk
    qi, kj, last = [], [], []
    for i in range(n_q):
        q_chunk_last = (q_pos0 + (i + 1) * tq - 1) // MASK_CHUNK
        k_last = min((q_chunk_last + 1) * MASK_CHUNK, tk_valid) - 1
        j_last = min(k_last // tk, n_k - 1)
        for j in range(j_last + 1):
            qi.append(i)
            kj.append(j)
            last.append(1 if j == j_last else 0)
    qi, kj, last = (jnp.asarray(a, jnp.int32) for a in (qi, kj, last))
    n_steps = int(qi.shape[0])

    q_spec = pl.BlockSpec((1, tq, 512), lambda b, s, qi, kj, la: (b, qi[s], 0))
    full = lambda shape: _resident(shape)
    grid_spec = pltpu.PrefetchScalarGridSpec(
        num_scalar_prefetch=3,
        grid=(bsz, n_steps),
        in_specs=[
            q_spec, q_spec,
            pl.BlockSpec((1, tk, 512), lambda b, s, qi, kj, la: (b, kj[s], 0)),
            pl.BlockSpec((1, N_HEADS * VT_ROWS, tk), lambda b, s, qi, kj, la: (b, 0, kj[s])),
            pl.BlockSpec((1, tq, d), lambda b, s, qi, kj, la: (b, qi[s], 0)),
            q_spec,
            pl.BlockSpec((1, 1, d), lambda b, s, qi, kj, la: (b, 0, 0)),
            full((4, D_DIFF)), full((LANES, 1)), full((d, d)),
        ],
        out_specs=pl.BlockSpec((1, tq, d), lambda b, s, qi, kj, la: (b, qi[s], 0)),
        scratch_shapes=[
            pltpu.VMEM((2 * N_HEADS, tq), F32),
            pltpu.VMEM((2 * N_HEADS, VT_ROWS, tq), F32),
        ],
    )
    return pl.pallas_call(
        functools.partial(_attn_kernel, tq=tq, tk=tk, q_pos0=q_pos0, tk_valid=tk_valid,
                          lam_init=lam_init, hi=hi),
        grid_spec=grid_spec,
        out_shape=jax.ShapeDtypeStruct((bsz, t_q, d), F32),
        compiler_params=_params(("parallel", "arbitrary")),
        name="diff_attention_hi" if hi else "diff_attention",
    )(qi, kj, last, q0, q1, kb, vt, x, yp, g1, lam_params, dnorm_g.reshape(LANES, 1),
      w_out.astype(_cdt(hi)))


def _route(h, rw, rb, hi):
    if hi:
        logits = _mm(h, rw, True)
    else:
        h_hi = h.astype(BF16)
        h_lo = (h - h_hi.astype(F32)).astype(BF16)
        both = jnp.dot(h_hi, rw, preferred_element_type=F32)
        logits = (both[:, :LANES] + both[:, LANES:]) + jnp.dot(h_lo, rw[:, :LANES], preferred_element_type=F32)
    s = jax.nn.sigmoid(logits)
    sel = s + rb
    lane_i = lax.broadcasted_iota(jnp.int32, sel.shape, 1)
    lane = lane_i.astype(F32)
    group = (lane_i >> 2).astype(F32)

    def top2(vals):
        m1 = jnp.max(vals, axis=1, keepdims=True)
        i1 = jnp.min(jnp.where(vals == m1, lane, float(N_EXPERTS)), axis=1, keepdims=True)
        rest = jnp.where(lane == i1, -jnp.inf, vals)
        m2 = jnp.max(rest, axis=1, keepdims=True)
        i2 = jnp.min(jnp.where(rest == m2, lane, float(N_EXPERTS)), axis=1, keepdims=True)
        return m1, i1, m2, i2

    best = None
    for g in range(N_EXPERTS // EPG):
        m1, _, m2, _ = top2(jnp.where(group == float(g), sel, -jnp.inf))
        score = m1 + m2
        if best is None:
            best, gbest = score, jnp.zeros_like(score)
        else:
            better = score > best
            best = jnp.where(better, score, best)
            gbest = jnp.where(better, float(g), gbest)
    _, i1, _, i2 = top2(jnp.where(group == gbest, sel, -jnp.inf))
    w1 = jnp.sum(jnp.where(lane == i1, s, 0.0), axis=1, keepdims=True)
    w2 = jnp.sum(jnp.where(lane == i2, s, 0.0), axis=1, keepdims=True)
    tot = w1 + w2
    return jnp.where(lane == i1, w1 / tot, 0.0) + jnp.where(lane == i2, w2 / tot, 0.0), gbest


def _moe_kernel(x_ref, sc_ref, sh_ref, g2_ref, ng_ref, rw_ref, rb_ref, w1_ref, w3_ref, w2_ref, fg_ref,
                o_ref, haug_scr, key_scr, keyt_scr, xg_scr, gg_scr, yg_scr, acc_scr, cnt_smem,
                *, final_norm, hi, rb_rows):
    step = pl.program_id(1)
    grp = step // STEPS_PER_GROUP
    tm, d = x_ref.shape[1], x_ref.shape[2]
    cdt = _cdt(hi)
    mm = functools.partial(_mm, hi=hi)
    lane = lax.broadcasted_iota(jnp.int32, (1, LANES), 1)

    @pl.when(step == 0)
    def _():
        h = _rms_mod(x_ref[0], ng_ref[...], sc_ref[0], sh_ref[0])
        gates, gbest = _route(h, rw_ref[...], rb_ref[...], hi)
        g_hi = gates.astype(BF16).astype(F32)
        rem = gates - g_hi
        g_mid = rem.astype(BF16).astype(F32)
        g_lo = (rem - g_mid).astype(BF16).astype(F32)
        pieces = g_hi + pltpu.roll(g_mid, N_EXPERTS, 1) + pltpu.roll(g_lo, 2 * N_EXPERTS, 1)
        haug_scr[:, 0:d] = h.astype(cdt)
        haug_scr[:, d:d + LANES] = pieces.astype(cdt)
        member = (gbest == lane.astype(F32)).astype(BF16)
        rows_i = lax.broadcasted_iota(jnp.int32, (tm, tm), 0)
        cols_i = lax.broadcasted_iota(jnp.int32, (tm, tm), 1)
        before = (cols_i < rows_i).astype(BF16)
        rank = jnp.dot(before, member, preferred_element_type=F32)
        key = jnp.where(member > 0, rank, -1.0)
        key_scr[...] = key
        eye = (lax.broadcasted_iota(jnp.int32, (8, LANES), 0)
               == lax.broadcasted_iota(jnp.int32, (8, LANES), 1)).astype(F32)
        keyt_scr[...] = _mm(eye, key, True, dims=_NT)
        for gi in range(N_EXPERTS // EPG):
            cnt_smem[gi] = jnp.sum(jnp.where(lane == gi, member.astype(F32), 0.0)).astype(jnp.int32)
        acc_scr[...] = jnp.zeros_like(acc_scr)

    n_blk = (cnt_smem[grp] + rb_rows - 1) // rb_rows

    @pl.when(step % STEPS_PER_GROUP == 0)
    def _():
        key_row = keyt_scr[pl.ds(grp, 1), :]

        def gather(blk, carry):
            off = pl.multiple_of(blk * rb_rows, rb_rows)
            row_id = (off + lax.broadcasted_iota(jnp.int32, (rb_rows, 1), 0)).astype(F32)
            onehot = (key_row == row_id).astype(cdt)
            xa = mm(onehot, haug_scr[...])
            xg_scr[pl.ds(off, rb_rows), :] = xa[:, 0:d].astype(cdt)
            pc = xa[:, d:d + LANES]
            gg_scr[pl.ds(off, rb_rows), :] = (pc + pltpu.roll(pc, LANES - N_EXPERTS, 1)
                                              + pltpu.roll(pc, LANES - 2 * N_EXPERTS, 1))
            yg_scr[pl.ds(off, rb_rows), :] = jnp.zeros((rb_rows, d), F32)
            return carry

        lax.fori_loop(0, n_blk, gather, 0)

    def expert(blk, carry):
        off = pl.multiple_of(blk * rb_rows, rb_rows)
        xb = xg_scr[pl.ds(off, rb_rows), :]
        gates = gg_scr[pl.ds(off, rb_rows), :]
        y = yg_scr[pl.ds(off, rb_rows), :]
        for j in range(EXPERTS_PER_STEP):
            a = mm(xb, w1_ref[j])
            b = mm(xb, w3_ref[j])
            ff = mm(a * jax.nn.sigmoid(a) * b, w2_ref[j])
            e = step * EXPERTS_PER_STEP + j
            y = y + jnp.sum(jnp.where(lane == e, gates, 0.0), axis=1, keepdims=True) * ff
        yg_scr[pl.ds(off, rb_rows), :] = y
        return carry

    lax.fori_loop(0, n_blk, expert, 0)

    @pl.when(step % STEPS_PER_GROUP == STEPS_PER_GROUP - 1)
    def _():
        key_col = jnp.sum(jnp.where(lane == grp, key_scr[...], 0.0), axis=1, keepdims=True)

        def scatter(blk, carry):
            off = pl.multiple_of(blk * rb_rows, rb_rows)
            col_id = (off + lax.broadcasted_iota(jnp.int32, (1, rb_rows), 1)).astype(F32)
            onehot_t = (key_col == col_id).astype(cdt)
            acc_scr[...] += mm(onehot_t, yg_scr[pl.ds(off, rb_rows), :])
            return carry

        lax.fori_loop(0, n_blk, scatter, 0)

    @pl.when(step == N_EXPERTS // EXPERTS_PER_STEP - 1)
    def _():
        y = x_ref[0] + g2_ref[0] * acc_scr[...]
        if final_norm:
            ms = jnp.mean(y * y, axis=-1, keepdims=True)
            y = y * lax.rsqrt(ms + EPS) * fg_ref[...]
        o_ref[0] = y


def _moe(x, sc, sh, g2, norm_g, router_w, router_b, w1, w3, w2, final_g, final_norm, tm, hi):
    bsz, t_len, d = x.shape
    n_t = t_len // tm
    rb_rows = min(256, tm)
    assert tm % rb_rows == 0
    rows = sc.shape[1]
    rw_pad = jnp.concatenate([router_w, jnp.zeros((d, LANES - N_EXPERTS), F32)], axis=1)
    if not hi:
        rw_hi = rw_pad.astype(BF16)
        rw_pad = jnp.concatenate([rw_hi, (rw_pad - rw_hi.astype(F32)).astype(BF16)], axis=1)
    rb_pad = jnp.concatenate([router_b, jnp.zeros((LANES - N_EXPERTS,), F32)]).reshape(1, LANES)
    if rows == 1:
        mod_spec = pl.BlockSpec((1, 1, d), lambda i, e: (i // n_t, 0, 0))
    else:
        mod_spec = pl.BlockSpec((1, tm, d), lambda i, e: (i // n_t, i % n_t, 0))
    x_spec = pl.BlockSpec((1, tm, d), lambda i, e: (i // n_t, i % n_t, 0))
    full = lambda shape: _resident(shape)
    dff = w1.shape[-1]
    return pl.pallas_call(
        functools.partial(_moe_kernel, final_norm=final_norm, hi=hi, rb_rows=rb_rows),
        grid=(bsz * n_t, N_EXPERTS // EXPERTS_PER_STEP),
        in_specs=[
            x_spec, mod_spec, mod_spec, mod_spec, full((1, d)),
            full(rw_pad.shape), full((1, LANES)),
            pl.BlockSpec((EXPERTS_PER_STEP, d, dff), lambda i, e: (e, 0, 0)),
            pl.BlockSpec((EXPERTS_PER_STEP, d, dff), lambda i, e: (e, 0, 0)),
            pl.BlockSpec((EXPERTS_PER_STEP, dff, d), lambda i, e: (e, 0, 0)),
            full((1, d)),
        ],
        out_specs=x_spec,
        out_shape=jax.ShapeDtypeStruct((bsz, t_len, d), F32),
        scratch_shapes=[
            pltpu.VMEM((tm, d + LANES), _cdt(hi)),
            pltpu.VMEM((tm, LANES), F32),
            pltpu.VMEM((8, tm), F32),
            pltpu.VMEM((tm, d), _cdt(hi)),
            pltpu.VMEM((tm, LANES), F32),
            pltpu.VMEM((tm, d), F32),
            pltpu.VMEM((tm, d), F32),
            pltpu.SMEM((N_EXPERTS // EPG,), jnp.int32),
        ],
        compiler_params=_params(("parallel", "arbitrary")),
        name="moe_hi" if hi else "moe",
    )(x, sc, sh, g2, norm_g.reshape(1, d), rw_pad, rb_pad, w1, w3, w2, final_g.reshape(1, d))


def _trunk(x, mods, pos0, t_valid, states, params, moe_weights, flat_moe, attn_tiles, proj_tm, moe_tm, hi):
    (norm_mix_g, norm_ffn_g, final_norm_g, w_in_even, i_bias, f_bias, mnorm_g, rnorm_g, w_out_even,
     w_in_odd, pool_w, pool_scale, lq1, lk1, lq2, lk2, dnorm_g, w_out_odd, router_w, router_b) = params
    c0, n0, m0, s0, pool_past, k_past, v_past = states
    bsz, t_pad, d = x.shape
    w1, w3, w2 = moe_weights

    def run_moe(x, layer, final):
        sc2, sh2, g2 = mods[layer][4], mods[layer][3], mods[layer][5]
        if flat_moe:
            n_rows = bsz * t_valid
            flat = lambda a: jnp.broadcast_to(a, (bsz, t_valid, d)).reshape(1, n_rows, d)
            y = _moe(x[:, :t_valid].reshape(1, n_rows, d), flat(sc2), flat(sh2), flat(g2), norm_ffn_g[layer],
                     router_w, router_b, w1[layer], w3[layer], w2[layer], final_norm_g, final, n_rows, hi)
            y = y.reshape(bsz, t_valid, d)
            return jnp.concatenate([y, jnp.zeros((bsz, t_pad - t_valid, d), F32)], axis=1)
        return _moe(x, sc2, sh2, g2, norm_ffn_g[layer], router_w, router_b,
                    w1[layer], w3[layer], w2[layer], final_norm_g, final, moe_tm, hi)

    sh1, sc1, g1 = mods[0][0], mods[0][1], mods[0][2]
    x, c_f, n_f, m_f, s_f = _even_mixer(x, sc1, sh1, g1, norm_mix_g[0], w_in_even[0], i_bias[0], f_bias[0],
                                        mnorm_g[0], rnorm_g[0], w_out_even[0], c0, n0, m0, s0, pos0, t_valid, hi)
    x = run_moe(x, 0, False)

    layer = 1
    lam_init = 0.8 - 0.6 * math.exp(-0.3 * layer)
    sh1, sc1, g1 = mods[1][0], mods[1][1], mods[1][2]
    k_rows, v_rows, kb, q0, q1, vt, yp, pool_new = _odd_proj(
        x, sc1, sh1, norm_mix_g[1], w_in_odd[0], pool_w[0], pool_scale[0], pool_past, pos0, t_valid, proj_tm, hi)
    if k_past is None:
        tk_valid = t_valid
    else:
        past_len = k_past.shape[1]
        tk_valid = past_len + t_valid
        kb = jnp.concatenate([k_past.reshape(bsz, past_len, 512).astype(kb.dtype), kb], axis=1)
        vt_past = jnp.concatenate([jnp.transpose(v_past, (0, 2, 3, 1)),
                                   jnp.ones((bsz, N_HEADS, VT_ROWS - DH, past_len), F32)], axis=2)
        vt = jnp.concatenate([vt_past.reshape(bsz, N_HEADS * VT_ROWS, past_len).astype(vt.dtype), vt], axis=2)
    lam_params = jnp.stack([lq1[0], lk1[0], lq2[0], lk2[0]])
    tq, tk = attn_tiles
    x = _attention(q0, q1, kb, vt, x, yp, g1, lam_params, dnorm_g[0], w_out_odd[0],
                   pos0, tk_valid, lam_init, tq, tk, hi)
    x = run_moe(x, 1, True)
    return x, (c_f, n_f, m_f, s_f), (pool_new[:, 1:], k_rows[:, :t_valid], v_rows[:, :t_valid])


def kernel(x_prompt, x_sample, c_prompt, c_sample, state_mlstm_c, state_mlstm_n, state_mlstm_m, state_ret, state_pool, cache_k, cache_v, ada_w, ada_b, norm_mix_g, norm_ffn_g, final_norm_g, w_in_even, mlstm_i_bias, mlstm_f_bias, mlstm_norm_g, ret_norm_g, w_out_even, w_in_odd, pool_w, pool_scale, lambda_q1, lambda_k1, lambda_q2, lambda_k2, diff_norm_g, w_out_odd, router_w, router_b, moe_w1, moe_w3, moe_w2):
    bp, tp, d = x_prompt.shape
    bs, ts, _ = x_sample.shape
    past_len = cache_k.shape[2]
    n_layers = ada_w.shape[0]

    c_all = jnp.concatenate([c_prompt, c_sample, jnp.zeros((16 - bp - bs, d), F32)], axis=0)
    mod_all = _adaln(c_all, ada_w, ada_b)

    def mods_for(lo, stop):
        return [[mod_all[l, lo:stop, None, k * d:(k + 1) * d] for k in range(6)] for l in range(n_layers)]

    params = (norm_mix_g, norm_ffn_g, final_norm_g, w_in_even, mlstm_i_bias, mlstm_f_bias,
              mlstm_norm_g.reshape(-1, 512), ret_norm_g.reshape(-1, 512), w_out_even,
              w_in_odd, pool_w, pool_scale, lambda_q1, lambda_k1, lambda_q2, lambda_k2, diff_norm_g,
              w_out_odd, router_w, router_b)

    zeros = lambda *s: jnp.zeros(s, F32)
    p_states = (zeros(bp, N_HEADS, DH, DH), zeros(bp, N_HEADS, DH), zeros(bp, N_HEADS),
                zeros(bp, N_HEADS, DH, DH), zeros(bp, POOL_PAD - 1, 512), None, None)
    moe_bf16 = (moe_w1.astype(BF16), moe_w3.astype(BF16), moe_w2.astype(BF16))
    y_p, ev_p, od_p = _trunk(x_prompt, mods_for(0, bp), 0, tp, p_states, params, moe_bf16,
                             flat_moe=False, attn_tiles=(min(1024, tp), min(512, tp)), proj_tm=min(512, tp),
                             moe_tm=min(1024, tp), hi=False)

    ts_pad = LANES
    x_s = jnp.concatenate([x_sample, zeros(bs, ts_pad - ts, d)], axis=1)
    s_states = (state_mlstm_c[0], state_mlstm_n[0], state_mlstm_m[0], state_ret[0], state_pool[0],
                cache_k[0], cache_v[0])
    y_s, ev_s, od_s = _trunk(x_s, mods_for(bp, bp + bs), past_len, ts, s_states, params,
                             (moe_w1, moe_w3, moe_w2), flat_moe=True,
                             attn_tiles=(ts_pad, past_len + ts_pad), proj_tm=ts_pad, moe_tm=None, hi=True)
    y_s = y_s[:, :ts]

    hd4 = lambda a: a.reshape(a.shape[0], a.shape[1], N_HEADS, DH)
    return (y_p, y_s,
            ev_p[0][None], ev_p[1][None], ev_p[2][None], ev_p[3][None],
            od_p[0][None], hd4(od_p[1])[None], hd4(od_p[2])[None],
            ev_s[0][None], ev_s[1][None], ev_s[2][None], ev_s[3][None],
            od_s[0][None], hd4(od_s[1])[None], hd4(od_s[2])[None])
```

```python
import functools
import math

import jax
import jax.numpy as jnp
from jax import lax
from jax.experimental import pallas as pl
from jax.experimental.pallas import tpu as pltpu

F32 = jnp.float32
BF16 = jnp.bfloat16

EPS = 1e-6
ROPE_THETA = 10000.0
MASK_CHUNK = 64
MASK_SHIFT = 6
DH = 128
N_HEADS = 4
D_DIFF = 64
POOL_WINDOWS = (2, 4, 8, 16)
POOL_PAD = 16
N_EXPERTS = 16
EPG = 4
EVEN_CHUNKS_PER_STEP = 4
EXPERTS_PER_STEP = 2
STEPS_PER_GROUP = EPG // EXPERTS_PER_STEP
MOE_ROW_BLOCK = 256
NEG_BIG = -1e30
LANES = 128
VT_ROWS = DH + 16
LOG2E = 1.4426950408889634
VMEM_LIMIT = 52 * 1024 * 1024

_NT = (((1,), (1,)), ((), ()))
_TN = (((0,), (0,)), ((), ()))


def _mm(a, b, hi, dims=None):
    if hi:
        a, b, prec = a.astype(F32), b.astype(F32), lax.Precision.HIGHEST
    else:
        a, b, prec = a.astype(BF16), b.astype(BF16), None
    if dims is None:
        return jnp.dot(a, b, precision=prec, preferred_element_type=F32)
    return lax.dot_general(a, b, dims, precision=prec, preferred_element_type=F32)


def _cdt(hi):
    return F32 if hi else BF16


def _rms_mod(x, g, sc, sh):
    ms = jnp.mean(x * x, axis=-1, keepdims=True)
    return (x * lax.rsqrt(ms + EPS) * g) * (1.0 + sc) + sh


def _params(sem):
    return pltpu.CompilerParams(dimension_semantics=sem, vmem_limit_bytes=VMEM_LIMIT)


def _resident(shape):
    zeros = (0,) * len(shape)
    return pl.BlockSpec(shape, lambda *args: zeros, pipeline_mode=pl.Buffered(1))


def _adaln_kernel(c_ref, w_ref, b_ref, o_ref):
    c = c_ref[...]
    o_ref[0] = _mm(c * jax.nn.sigmoid(c), w_ref[0], True) + b_ref[0]


def _adaln(c_all, ada_w, ada_b):
    n_layers, d, n6 = ada_w.shape
    rows = c_all.shape[0]
    bn = 1536
    return pl.pallas_call(
        _adaln_kernel,
        grid=(n_layers, n6 // bn),
        in_specs=[
            pl.BlockSpec((rows, d), lambda l, j: (0, 0)),
            pl.BlockSpec((1, d, bn), lambda l, j: (l, 0, j)),
            pl.BlockSpec((1, 1, bn), lambda l, j: (l, 0, j)),
        ],
        out_specs=pl.BlockSpec((1, rows, bn), lambda l, j: (l, 0, j)),
        out_shape=jax.ShapeDtypeStruct((n_layers, rows, n6), F32),
        compiler_params=_params(("parallel", "parallel")),
        name="adaln",
    )(c_all, ada_w, ada_b.reshape(n_layers, 1, n6))


def _log_sigmoid(x):
    return jnp.minimum(x, 0.0) - jnp.log1p(jnp.exp(-jnp.abs(x)))


def _even_kernel(x_ref, sc_ref, sh_ref, g1_ref, ng_ref, wrow_ref, wkt_ref, wgt_ref, gb_ref,
                 cosr_ref, sinr_ref, cost_ref, sint_ref, mng_ref, rng_ref, wout_ref,
                 c0_ref, m0_ref, s0_ref,
                 xo_ref, cf_ref, mf_ref, sf_ref,
                 r_scr, kt_scr, c_scr, m_scr, s_scr, y_scr, *, lc, n_c, lv, hi):
    t = pl.program_id(1)
    n_t = pl.num_programs(1)
    mm = functools.partial(_mm, hi=hi)

    @pl.when(t == 0)
    def _():
        c_scr[...] = c0_ref[0]
        m_scr[...] = m0_ref[0]
        s_scr[...] = s0_ref[0]

    x = x_ref[0]
    h = _rms_mod(x, ng_ref[...], sc_ref[0], sh_ref[0]).astype(_cdt(hi))
    r_scr[...] = mm(h, wrow_ref[...])
    kt_scr[...] = mm(wkt_ref[...], h, dims=_NT) * (DH ** -0.5)
    gt = mm(wgt_ref[...], h, dims=_NT) + gb_ref[...]

    lane = lax.broadcasted_iota(jnp.int32, (1, lc), 1)
    valid = lane < lv
    row_i = lax.broadcasted_iota(jnp.int32, (lc, lc), 0)
    col_i = lax.broadcasted_iota(jnp.int32, (lc, lc), 1)
    tri = col_i <= row_i
    eye = col_i == row_i
    upper = (row_i <= col_i).astype(F32)
    rel = (row_i - col_i).astype(F32)
    jrow = lane.astype(F32)
    jcol = lax.broadcasted_iota(jnp.int32, (lc, 1), 0).astype(F32)
    ones_col = (lax.broadcasted_iota(jnp.int32, (lc, DH), 1) == 0).astype(_cdt(hi))

    for c in range(n_c):
        rs = slice(c * lc, (c + 1) * lc)
        ig = gt[0:N_HEADS, rs]
        lf = jnp.where(valid, _log_sigmoid(gt[N_HEADS:2 * N_HEADS, rs]), 0.0)
        b_all = _mm(lf, upper, True)
        u_all = jnp.where(valid, ig - b_all, -jnp.inf)

        for hd in range(N_HEADS):
            sl = slice(hd * DH, (hd + 1) * DH)
            q = r_scr[rs, sl].astype(_cdt(hi))
            v = r_scr[rs, 512 + hd * DH:512 + (hd + 1) * DH].astype(_cdt(hi))
            vaug = jnp.concatenate([v, ones_col], axis=1)
            kt = kt_scr[sl, rs]
            u_row = u_all[hd:hd + 1]
            b_row = b_all[hd:hd + 1]
            m_prev = m_scr[hd]
            caug = c_scr[hd]

            u_mat = jnp.where(tri, u_row, -jnp.inf)
            r_col = jnp.maximum(m_prev, jnp.max(u_mat, axis=1, keepdims=True))
            b_col = jnp.sum(jnp.where(eye, b_row, 0.0), axis=1, keepdims=True)
            s_qk = mm(q, kt)
            p = jnp.exp(u_mat - r_col) * s_qk
            sc_col = jnp.exp(m_prev - r_col)
            nd = sc_col * mm(q, caug) + mm(p, vaug)
            num = nd[:, :DH]
            den = nd[:, DH:DH + 1]
            m_t = b_col + r_col
            h_m = num / jnp.maximum(jnp.abs(den), jnp.exp(-m_t))
            ms = jnp.mean(h_m * h_m, axis=-1, keepdims=True)
            h_m = h_m * lax.rsqrt(ms + EPS) * mng_ref[:, sl]
            gate = jax.nn.sigmoid(r_scr[rs, 1024 + hd * DH:1024 + (hd + 1) * DH])
            y_scr[rs, sl] = (gate * h_m).astype(y_scr.dtype)

            u_max = jnp.max(u_row, axis=1, keepdims=True)
            w_row = jnp.exp(u_row - u_max)
            kv = mm(kt * w_row, vaug)
            b_end = b_row[:, lc - 1:lc]
            a_max = b_end + u_max
            m_new = jnp.maximum(b_end + m_prev, a_max)
            c_scr[hd] = jnp.exp(b_end + m_prev - m_new) * caug + jnp.exp(a_max - m_new) * kv
            m_scr[hd] = m_new

        cos_r = cosr_ref[rs, :]
        sin_r = sinr_ref[rs, :]
        cos_t = cost_ref[:, rs]
        sin_t = sint_ref[:, rs]
        for hd in range(N_HEADS):
            lg = math.log(1.0 - 2.0 ** (-5.0 - hd))
            sl = slice(hd * DH, (hd + 1) * DH)
            rq = r_scr[rs, 1536 + hd * DH:1536 + (hd + 1) * DH]
            rq = (rq * cos_r + pltpu.roll(rq, DH // 2, 1) * sin_r).astype(_cdt(hi))
            rv = r_scr[rs, 2048 + hd * DH:2048 + (hd + 1) * DH].astype(_cdt(hi))
            kt = kt_scr[512 + hd * DH:512 + (hd + 1) * DH, rs]
            top, bot = kt[:DH // 2], kt[DH // 2:]
            kt = jnp.concatenate([top * cos_t - bot * sin_t, top * sin_t + bot * cos_t], axis=0)
            s_prev = s_scr[hd]

            decay = jnp.where(rel >= 0, jnp.exp(lg * jnp.maximum(rel, 0.0)), 0.0)
            scores = mm(rq, kt) * decay
            intra = mm(scores, rv)
            inter = mm(rq, s_prev) * jnp.exp(lg * (jcol + 1.0))
            o = intra + inter
            ms = jnp.mean(o * o, axis=-1, keepdims=True)
            o = o * lax.rsqrt(ms + EPS) * rng_ref[:, sl]
            g_in = r_scr[rs, 2560 + hd * DH:2560 + (hd + 1) * DH]
            y_scr[rs, 512 + hd * DH:512 + (hd + 1) * DH] = (g_in * jax.nn.sigmoid(g_in) * o).astype(y_scr.dtype)

            w_end = jnp.where(valid, jnp.exp(lg * (lv - 1.0 - jrow)), 0.0)
            s_scr[hd] = math.exp(lg * lv) * s_prev + mm(kt * w_end, rv)

    xo_ref[0] = x + g1_ref[0] * mm(y_scr[...], wout_ref[...])

    @pl.when(t == n_t - 1)
    def _():
        cf_ref[0] = c_scr[...]
        mf_ref[0] = m_scr[...]
        sf_ref[0] = s_scr[...]


def _even_mixer(x, sc, sh, g1, norm_g, w_in, i_bias, f_bias, mnorm_g, rnorm_g, w_out,
                c0, n0, m0, s0, pos0, t_valid, hi):
    bsz, t_pad, d = x.shape
    lc = LANES
    n_c = max(1, min(EVEN_CHUNKS_PER_STEP, t_pad // lc))
    tm = n_c * lc
    n_t = t_pad // tm
    lv = lc if t_pad > lc else t_valid
    assert t_pad == lc or t_valid == t_pad
    wdt = _cdt(hi)
    w = w_in
    cols = lambda a, b: w[:, a:b]
    w_row = jnp.concatenate([cols(0, 512), cols(1024, 1536), cols(1536, 2048),
                             cols(2056, 2568), cols(3080, 3592), cols(3592, 4104)], axis=1).astype(wdt)
    w_kt = jnp.concatenate([cols(512, 1024), cols(2568, 3080)], axis=1).T.astype(wdt)
    w_gt = jnp.concatenate([cols(2048, 2056).T, jnp.zeros((8, d), F32)], axis=0).astype(wdt)
    gb = jnp.concatenate([i_bias, f_bias, jnp.zeros((8,), F32)]).reshape(16, 1)

    half = DH // 2
    inv = jnp.power(ROPE_THETA, -jnp.arange(half, dtype=F32) * 2.0 / DH)
    pos = pos0 + jnp.arange(t_pad, dtype=jnp.int32)
    ang = pos.astype(F32)[:, None] * inv[None, :]
    cos, sin = jnp.cos(ang), jnp.sin(ang)
    cos_r = jnp.concatenate([cos, cos], axis=1)
    sin_r = jnp.concatenate([-sin, sin], axis=1)
    cos_t, sin_t = cos.T, sin.T

    caug0 = jnp.concatenate([c0, n0[..., None], jnp.zeros(c0.shape[:-1] + (DH - 1,), F32)], axis=-1)
    m0 = m0.reshape(bsz, N_HEADS, 1, 1)

    full = lambda shape: _resident(shape)
    per_b = lambda shape: pl.BlockSpec((1,) + shape, lambda b, t: (b,) + (0,) * len(shape))
    outs = pl.pallas_call(
        functools.partial(_even_kernel, lc=lc, n_c=n_c, lv=lv, hi=hi),
        grid=(bsz, n_t),
        in_specs=[
            pl.BlockSpec((1, tm, d), lambda b, t: (b, t, 0)),
            per_b((1, d)), per_b((1, d)), per_b((1, d)),
            full((1, d)),
            full((d, 3072)), full((1024, d)), full((16, d)), full((16, 1)),
            pl.BlockSpec((tm, DH), lambda b, t: (t, 0)),
            pl.BlockSpec((tm, DH), lambda b, t: (t, 0)),
            pl.BlockSpec((half, tm), lambda b, t: (0, t)),
            pl.BlockSpec((half, tm), lambda b, t: (0, t)),
            full((1, 512)), full((1, 512)), full((d, d)),
            per_b((N_HEADS, DH, 2 * DH)), per_b((N_HEADS, 1, 1)), per_b((N_HEADS, DH, DH)),
        ],
        out_specs=[
            pl.BlockSpec((1, tm, d), lambda b, t: (b, t, 0)),
            per_b((N_HEADS, DH, 2 * DH)), per_b((N_HEADS, 1, 1)), per_b((N_HEADS, DH, DH)),
        ],
        out_shape=[
            jax.ShapeDtypeStruct((bsz, t_pad, d), F32),
            jax.ShapeDtypeStruct((bsz, N_HEADS, DH, 2 * DH), F32),
            jax.ShapeDtypeStruct((bsz, N_HEADS, 1, 1), F32),
            jax.ShapeDtypeStruct((bsz, N_HEADS, DH, DH), F32),
        ],
        scratch_shapes=[
            pltpu.VMEM((tm, 3072), F32),
            pltpu.VMEM((1024, tm), F32),
            pltpu.VMEM((N_HEADS, DH, 2 * DH), F32),
            pltpu.VMEM((N_HEADS, 1, 1), F32),
            pltpu.VMEM((N_HEADS, DH, DH), F32),
            pltpu.VMEM((tm, d), wdt),
        ],
        compiler_params=_params(("parallel", "arbitrary")),
        name="even_mixer_hi" if hi else "even_mixer",
    )(x, sc, sh, g1, norm_g.reshape(1, d), w_row, w_kt, w_gt, gb,
      cos_r, sin_r, cos_t, sin_t, mnorm_g.reshape(1, 512), rnorm_g.reshape(1, 512),
      w_out.astype(wdt), caug0, m0, s0)
    x_new, caug, m_f, s_f = outs
    return x_new, caug[..., :DH], caug[..., DH], m_f.reshape(bsz, N_HEADS), s_f


def _odd_proj_kernel(x_ref, sc_ref, sh_ref, ng_ref, w_ref, wvt_ref, cos_ref, sin_ref,
                     pw_ref, ps_ref, past_ref,
                     krow_ref, vrow_ref, kb_ref, q0_ref, q1_ref, vt_ref, yp_ref, pool_ref,
                     r_scr, ext_scr, *, tm, tv, pos0, hi):
    t = pl.program_id(1)
    mm = functools.partial(_mm, hi=hi)

    @pl.when(t == 0)
    def _():
        ext_scr[0:POOL_PAD] = past_ref[0]

    x = x_ref[0]
    h = _rms_mod(x, ng_ref[...], sc_ref[0], sh_ref[0]).astype(_cdt(hi))
    r_scr[...] = mm(h, w_ref[...])
    vt = mm(wvt_ref[...], h, dims=_NT)
    for hd in range(N_HEADS):
        vt_ref[0, hd * VT_ROWS:hd * VT_ROWS + DH, :] = vt[hd * DH:(hd + 1) * DH].astype(vt_ref.dtype)
        vt_ref[0, hd * VT_ROWS + DH:(hd + 1) * VT_ROWS, :] = jnp.ones((VT_ROWS - DH, tm), vt_ref.dtype)
        vrow_ref[0, :, hd, :] = r_scr[:, 1536 + hd * DH:1536 + (hd + 1) * DH]

    ext_scr[POOL_PAD:POOL_PAD + tm] = r_scr[:, 0:512]
    pos = (pos0 + t * tm + lax.broadcasted_iota(jnp.int32, (tm, 1), 0)).astype(F32)
    for g, w in enumerate(POOL_WINDOWS):
        sl = slice(g * LANES, (g + 1) * LANES)
        pin = r_scr[:, sl]
        win = pin
        for j in range(1, w):
            win = win + ext_scr[POOL_PAD - j:POOL_PAD - j + tm, sl]
        cnt = jnp.minimum(pos + 1.0, float(w))
        dev = win / cnt - pin
        yp_ref[0, :, sl] = (mm(dev, pw_ref[g]) * ps_ref[:, sl]).astype(yp_ref.dtype)
    pool_ref[0] = ext_scr[tv:tv + POOL_PAD]
    ext_scr[0:POOL_PAD] = ext_scr[tm:tm + POOL_PAD]

    lane = lax.broadcasted_iota(jnp.int32, (1, LANES), 1)
    low_half = (lane & (D_DIFF - 1)) < (D_DIFF // 2)
    comp0 = lane < D_DIFF
    cos = cos_ref[...]
    sin = sin_ref[...]

    def rope(a):
        swapped = jnp.where(low_half, pltpu.roll(a, LANES - D_DIFF // 2, 1), pltpu.roll(a, D_DIFF // 2, 1))
        return a * cos + swapped * sin

    for hd in range(N_HEADS):
        sl = slice(hd * LANES, (hd + 1) * LANES)
        q = rope(r_scr[:, 512 + hd * LANES:512 + (hd + 1) * LANES]) * (D_DIFF ** -0.5 * LOG2E)
        q0_ref[0, :, sl] = jnp.where(comp0, q, 0.0).astype(q0_ref.dtype)
        q1_ref[0, :, sl] = jnp.where(comp0, 0.0, q).astype(q1_ref.dtype)
        k = rope(r_scr[:, 1024 + hd * LANES:1024 + (hd + 1) * LANES])
        krow_ref[0, :, hd, :] = k
        kb_ref[0, :, sl] = k.astype(kb_ref.dtype)


def _odd_proj(x, sc, sh, norm_g, w_in, pool_w, pool_scale, pool_past, pos0, t_valid, tm, hi):
    bsz, t_pad, d = x.shape
    n_t = t_pad // tm
    tv = tm if n_t > 1 else t_valid
    assert n_t == 1 or t_valid == t_pad
    wdt = _cdt(hi)
    w_all = w_in.astype(wdt)
    w_vt = w_in[:, 1536:2048].T.astype(wdt)

    half = D_DIFF // 2
    inv = jnp.power(ROPE_THETA, -jnp.arange(half, dtype=F32) * 2.0 / D_DIFF)
    pos = pos0 + jnp.arange(t_pad, dtype=jnp.int32)
    ang = pos.astype(F32)[:, None] * inv[None, :]
    cos, sin = jnp.cos(ang), jnp.sin(ang)
    cos_t = jnp.concatenate([cos, cos, cos, cos], axis=1)
    sin_t = jnp.concatenate([-sin, sin, -sin, sin], axis=1)
    past = jnp.concatenate([jnp.zeros((bsz, 1, 512), F32), pool_past], axis=1)

    full = lambda shape: _resident(shape)
    per_b = lambda shape: pl.BlockSpec((1,) + shape, lambda b, t: (b,) + (0,) * len(shape))
    tile = lambda n: pl.BlockSpec((1, tm, n), lambda b, t: (b, t, 0))
    head_tile = pl.BlockSpec((1, tm, N_HEADS, DH), lambda b, t: (b, t, 0, 0))
    f32_rows = jax.ShapeDtypeStruct((bsz, t_pad, N_HEADS, DH), F32)
    c_rows = jax.ShapeDtypeStruct((bsz, t_pad, 512), wdt)
    return pl.pallas_call(
        functools.partial(_odd_proj_kernel, tm=tm, tv=tv, pos0=pos0, hi=hi),
        grid=(bsz, n_t),
        in_specs=[
            tile(d), per_b((1, d)), per_b((1, d)), full((1, d)),
            full((d, 2048)), full((512, d)),
            pl.BlockSpec((tm, LANES), lambda b, t: (t, 0)),
            pl.BlockSpec((tm, LANES), lambda b, t: (t, 0)),
            full((4, LANES, LANES)), full((1, 512)), per_b((POOL_PAD, 512)),
        ],
        out_specs=[
            head_tile, head_tile, tile(512), tile(512), tile(512),
            pl.BlockSpec((1, N_HEADS * VT_ROWS, tm), lambda b, t: (b, 0, t)),
            tile(512), per_b((POOL_PAD, 512)),
        ],
        out_shape=[f32_rows, f32_rows, c_rows, c_rows, c_rows,
                   jax.ShapeDtypeStruct((bsz, N_HEADS * VT_ROWS, t_pad), wdt), c_rows,
                   jax.ShapeDtypeStruct((bsz, POOL_PAD, 512), F32)],
        scratch_shapes=[pltpu.VMEM((tm, 2048), F32), pltpu.VMEM((tm + POOL_PAD, 512), F32)],
        compiler_params=_params(("parallel", "arbitrary")),
        name="odd_proj_hi" if hi else "odd_proj",
    )(x, sc, sh, norm_g.reshape(1, d), w_all, w_vt, cos_t, sin_t,
      pool_w.astype(wdt), pool_scale.reshape(1, 512), past)


def _attn_kernel(qi_ref, kj_ref, last_ref,
                 q0_ref, q1_ref, kb_ref, vt_ref, x_ref, yp_ref, g1_ref, lamp_ref, dng_ref, wout_ref,
                 xo_ref, m_scr, acc_scr, *, tq, tk, q_pos0, tk_valid, lam_init, hi):
    s = pl.program_id(1)
    qi = qi_ref[s]
    kj = kj_ref[s]
    mm = functools.partial(_mm, hi=hi)

    @pl.when(kj == 0)
    def _():
        m_scr[...] = jnp.full_like(m_scr, NEG_BIG)
        acc_scr[...] = jnp.zeros_like(acc_scr)

    q_refs = (q0_ref, q1_ref)

    def scores(i):
        sl = slice((i // 2) * LANES, (i // 2 + 1) * LANES)
        return mm(kb_ref[0, :, sl], q_refs[i % 2][0, :, sl], dims=_NT)

    def step(masked):
        if masked:
            kpos = kj * tk + lax.broadcasted_iota(jnp.int32, (tk, tq), 0)
            qpos = q_pos0 + qi * tq + lax.broadcasted_iota(jnp.int32, (tk, tq), 1)
            visible = ((kpos >> MASK_SHIFT) <= (qpos >> MASK_SHIFT)) & (kpos < tk_valid)
        st_next = scores(0)
        for i in range(2 * N_HEADS):
            st = st_next
            if i + 1 < 2 * N_HEADS:
                st_next = scores(i + 1)
            if masked:
                st = jnp.where(visible, st, NEG_BIG)
            hd = i // 2
            vt = vt_ref[0, hd * VT_ROWS:(hd + 1) * VT_ROWS, :]
            m_prev = m_scr[i:i + 1]
            m_new = jnp.maximum(m_prev, jnp.max(st, axis=0, keepdims=True))
            alpha = jnp.exp2(m_prev - m_new)
            p = jnp.exp2(st - m_new)
            acc_scr[i] = alpha * acc_scr[i] + mm(vt, p)
            m_scr[i:i + 1] = m_new

    fully_visible = ((kj + 1) * tk - 1) // MASK_CHUNK <= (q_pos0 + qi * tq) // MASK_CHUNK
    no_pad = (kj + 1) * tk <= tk_valid
    clean = jnp.logical_and(fully_visible, no_pad)
    pl.when(clean)(lambda: step(False))
    pl.when(jnp.logical_not(clean))(lambda: step(True))

    @pl.when(last_ref[s] == 1)
    def _():
        lp = lamp_ref[...]
        lam = (jnp.exp(jnp.sum(lp[0:1] * lp[1:2], axis=1, keepdims=True))
               - jnp.exp(jnp.sum(lp[2:3] * lp[3:4], axis=1, keepdims=True)) + lam_init)
        y = mm(yp_ref[0], wout_ref[0:512, :])
        for hd in range(N_HEADS):
            o0 = acc_scr[2 * hd, 0:DH] / acc_scr[2 * hd, DH:DH + 1]
            o1 = acc_scr[2 * hd + 1, 0:DH] / acc_scr[2 * hd + 1, DH:DH + 1]
            ot = o0 - lam * o1
            ms = jnp.mean(ot * ot, axis=0, keepdims=True)
            ot = (ot * lax.rsqrt(ms + EPS) * dng_ref[...]) * (1.0 - lam_init)
            y = y + mm(ot, wout_ref[512 + hd * LANES:512 + (hd + 1) * LANES, :], dims=_TN)
        xo_ref[0] = x_ref[0] + g1_ref[0] * y


def _attention(q0, q1, kb, vt, x, yp, g1, lam_params, dnorm_g, w_out, q_pos0, tk_valid, lam_init, tq, tk, hi):
    bsz, t_q, d = x.shape
    t_k = kb.shape[1]
    n_q, n_k = t_q // tq, t_k // tk
    qi, kj, last = [], [], []
    for i in range(n_q):
        q_chunk_last = (q_pos0 + (i + 1) * tq - 1) // MASK_CHUNK
        k_last = min((q_chunk_last + 1) * MASK_CHUNK, tk_valid) - 1
        j_last = min(k_last // tk, n_k - 1)
        for j in range(j_last + 1):
            qi.append(i)
            kj.append(j)
            last.append(1 if j == j_last else 0)
    qi, kj, last = (jnp.asarray(a, jnp.int32) for a in (qi, kj, last))
    n_steps = int(qi.shape[0])

    q_spec = pl.BlockSpec((1, tq, 512), lambda b, s, qi, kj, la: (b, qi[s], 0))
    full = lambda shape: _resident(shape)
    grid_spec = pltpu.PrefetchScalarGridSpec(
        num_scalar_prefetch=3,
        grid=(bsz, n_steps),
        in_specs=[
            q_spec, q_spec,
            pl.BlockSpec((1, tk, 512), lambda b, s, qi, kj, la: (b, kj[s], 0)),
            pl.BlockSpec((1, N_HEADS * VT_ROWS, tk), lambda b, s, qi, kj, la: (b, 0, kj[s])),
            pl.BlockSpec((1, tq, d), lambda b, s, qi, kj, la: (b, qi[s], 0)),
            q_spec,
            pl.BlockSpec((1, 1, d), lambda b, s, qi, kj, la: (b, 0, 0)),
            full((4, D_DIFF)), full((LANES, 1)), full((d, d)),
        ],
        out_specs=pl.BlockSpec((1, tq, d), lambda b, s, qi, kj, la: (b, qi[s], 0)),
        scratch_shapes=[
            pltpu.VMEM((2 * N_HEADS, tq), F32),
            pltpu.VMEM((2 * N_HEADS, VT_ROWS, tq), F32),
        ],
    )
    return pl.pallas_call(
        functools.partial(_attn_kernel, tq=tq, tk=tk, q_pos0=q_pos0, tk_valid=tk_valid,
                          lam_init=lam_init, hi=hi),
        grid_spec=grid_spec,
        out_shape=jax.ShapeDtypeStruct((bsz, t_q, d), F32),
        compiler_params=_params(("parallel", "arbitrary")),
        name="diff_attention_hi" if hi else "diff_attention",
    )(qi, kj, last, q0, q1, kb, vt, x, yp, g1, lam_params, dnorm_g.reshape(LANES, 1),
      w_out.astype(_cdt(hi)))


def _route(h, rw, rb, hi):
    if hi:
        logits = _mm(h, rw, True)
    else:
        h_hi = h.astype(BF16)
        h_lo = (h - h_hi.astype(F32)).astype(BF16)
        both = jnp.dot(h_hi, rw, preferred_element_type=F32)
        logits = (both[:, :LANES] + both[:, LANES:]) + jnp.dot(h_lo, rw[:, :LANES], preferred_element_type=F32)
    s = jax.nn.sigmoid(logits)
    sel = s + rb
    lane_i = lax.broadcasted_iota(jnp.int32, sel.shape, 1)
    lane = lane_i.astype(F32)
    group = (lane_i >> 2).astype(F32)

    def top2(vals):
        m1 = jnp.max(vals, axis=1, keepdims=True)
        i1 = jnp.min(jnp.where(vals == m1, lane, float(N_EXPERTS)), axis=1, keepdims=True)
        rest = jnp.where(lane == i1, -jnp.inf, vals)
        m2 = jnp.max(rest, axis=1, keepdims=True)
        i2 = jnp.min(jnp.where(rest == m2, lane, float(N_EXPERTS)), axis=1, keepdims=True)
        return m1, i1, m2, i2

    best = None
    for g in range(N_EXPERTS // EPG):
        m1, _, m2, _ = top2(jnp.where(group == float(g), sel, -jnp.inf))
        score = m1 + m2
        if best is None:
            best, gbest = score, jnp.zeros_like(score)
        else:
            better = score > best
            best = jnp.where(better, score, best)
            gbest = jnp.where(better, float(g), gbest)
    _, i1, _, i2 = top2(jnp.where(group == gbest, sel, -jnp.inf))
    w1 = jnp.sum(jnp.where(lane == i1, s, 0.0), axis=1, keepdims=True)
    w2 = jnp.sum(jnp.where(lane == i2, s, 0.0), axis=1, keepdims=True)
    tot = w1 + w2
    return jnp.where(lane == i1, w1 / tot, 0.0) + jnp.where(lane == i2, w2 / tot, 0.0), gbest


def _moe_kernel(x_ref, sc_ref, sh_ref, g2_ref, ng_ref, rw_ref, rb_ref, w1_ref, w3_ref, w2_ref, fg_ref,
                o_ref, haug_scr, key_scr, keyt_scr, xg_scr, gg_scr, yg_scr, acc_scr, cnt_smem,
                *, final_norm, hi, rb_rows):
    step = pl.program_id(1)
    grp = step // STEPS_PER_GROUP
    tm, d = x_ref.shape[1], x_ref.shape[2]
    cdt = _cdt(hi)
    mm = functools.partial(_mm, hi=hi)
    lane = lax.broadcasted_iota(jnp.int32, (1, LANES), 1)

    @pl.when(step == 0)
    def _():
        h = _rms_mod(x_ref[0], ng_ref[...], sc_ref[0], sh_ref[0])
        gates, gbest = _route(h, rw_ref[...], rb_ref[...], hi)
        g_hi = gates.astype(BF16).astype(F32)
        rem = gates - g_hi
        g_mid = rem.astype(BF16).astype(F32)
        g_lo = (rem - g_mid).astype(BF16).astype(F32)
        pieces = g_hi + pltpu.roll(g_mid, N_EXPERTS, 1) + pltpu.roll(g_lo, 2 * N_EXPERTS, 1)
        haug_scr[:, 0:d] = h.astype(cdt)
        haug_scr[:, d:d + LANES] = pieces.astype(cdt)
        member = (gbest == lane.astype(F32)).astype(BF16)
        rows_i = lax.broadcasted_iota(jnp.int32, (tm, tm), 0)
        cols_i = lax.broadcasted_iota(jnp.int32, (tm, tm), 1)
        before = (cols_i < rows_i).astype(BF16)
        rank = jnp.dot(before, member, preferred_element_type=F32)
        key = jnp.where(member > 0, rank, -1.0)
        key_scr[...] = key
        eye = (lax.broadcasted_iota(jnp.int32, (8, LANES), 0)
               == lax.broadcasted_iota(jnp.int32, (8, LANES), 1)).astype(F32)
        keyt_scr[...] = _mm(eye, key, True, dims=_NT)
        for gi in range(N_EXPERTS // EPG):
            cnt_smem[gi] = jnp.sum(jnp.where(lane == gi, member.astype(F32), 0.0)).astype(jnp.int32)
        acc_scr[...] = jnp.zeros_like(acc_scr)

    n_blk = (cnt_smem[grp] + rb_rows - 1) // rb_rows

    @pl.when(step % STEPS_PER_GROUP == 0)
    def _():
        key_row = keyt_scr[pl.ds(grp, 1), :]

        def gather(blk, carry):
            off = pl.multiple_of(blk * rb_rows, rb_rows)
            row_id = (off + lax.broadcasted_iota(jnp.int32, (rb_rows, 1), 0)).astype(F32)
            onehot = (key_row == row_id).astype(cdt)
            xa = mm(onehot, haug_scr[...])
            xg_scr[pl.ds(off, rb_rows), :] = xa[:, 0:d].astype(cdt)
            pc = xa[:, d:d + LANES]
            gg_scr[pl.ds(off, rb_rows), :] = (pc + pltpu.roll(pc, LANES - N_EXPERTS, 1)
                                              + pltpu.roll(pc, LANES - 2 * N_EXPERTS, 1))
            yg_scr[pl.ds(off, rb_rows), :] = jnp.zeros((rb_rows, d), F32)
            return carry

        lax.fori_loop(0, n_blk, gather, 0)

    def expert(blk, carry):
        off = pl.multiple_of(blk * rb_rows, rb_rows)
        xb = xg_scr[pl.ds(off, rb_rows), :]
        gates = gg_scr[pl.ds(off, rb_rows), :]
        y = yg_scr[pl.ds(off, rb_rows), :]
        for j in range(EXPERTS_PER_STEP):
            a = mm(xb, w1_ref[j])
            b = mm(xb, w3_ref[j])
            ff = mm(a * jax.nn.sigmoid(a) * b, w2_ref[j])
            e = step * EXPERTS_PER_STEP + j
            y = y + jnp.sum(jnp.where(lane == e, gates, 0.0), axis=1, keepdims=True) * ff
        yg_scr[pl.ds(off, rb_rows), :] = y
        return carry

    lax.fori_loop(0, n_blk, expert, 0)

    @pl.when(step % STEPS_PER_GROUP == STEPS_PER_GROUP - 1)
    def _():
        key_col = jnp.sum(jnp.where(lane == grp, key_scr[...], 0.0), axis=1, keepdims=True)

        def scatter(blk, carry):
            off = pl.multiple_of(blk * rb_rows, rb_rows)
            col_id = (off + lax.broadcasted_iota(jnp.int32, (1, rb_rows), 1)).astype(F32)
            onehot_t = (key_col == col_id).astype(cdt)
            acc_scr[...] += mm(onehot_t, yg_scr[pl.ds(off, rb_rows), :])
            return carry

        lax.fori_loop(0, n_blk, scatter, 0)

    @pl.when(step == N_EXPERTS // EXPERTS_PER_STEP - 1)
    def _():
        y = x_ref[0] + g2_ref[0] * acc_scr[...]
        if final_norm:
            ms = jnp.mean(y * y, axis=-1, keepdims=True)
            y = y * lax.rsqrt(ms + EPS) * fg_ref[...]
        o_ref[0] = y


def _moe(x, sc, sh, g2, norm_g, router_w, router_b, w1, w3, w2, layer, final_g, final_norm, tm, hi):
    bsz, t_len, d = x.shape
    n_t = t_len // tm
    if hi:
        rb_rows = min(tm, -(-(tm * 5 // 16) // 16) * 16)
    else:
        rb_rows = min(tm, MOE_ROW_BLOCK)
    cap_rows = -(-tm // rb_rows) * rb_rows
    rows = sc.shape[1]
    lw = layer * (N_EXPERTS // EXPERTS_PER_STEP)
    rw_pad = jnp.concatenate([router_w, jnp.zeros((d, LANES - N_EXPERTS), F32)], axis=1)
    if not hi:
        rw_hi = rw_pad.astype(BF16)
        rw_pad = jnp.concatenate([rw_hi, (rw_pad - rw_hi.astype(F32)).astype(BF16)], axis=1)
    rb_pad = jnp.concatenate([router_b, jnp.zeros((LANES - N_EXPERTS,), F32)]).reshape(1, LANES)
    if rows == 1:
        mod_spec = pl.BlockSpec((1, 1, d), lambda i, e: (i // n_t, 0, 0))
    else:
        mod_spec = pl.BlockSpec((1, tm, d), lambda i, e: (i // n_t, i % n_t, 0))
    x_spec = pl.BlockSpec((1, tm, d), lambda i, e: (i // n_t, i % n_t, 0))
    full = lambda shape: _resident(shape)
    dff = w1.shape[-1]
    return pl.pallas_call(
        functools.partial(_moe_kernel, final_norm=final_norm, hi=hi, rb_rows=rb_rows),
        grid=(bsz * n_t, N_EXPERTS // EXPERTS_PER_STEP),
        in_specs=[
            x_spec, mod_spec, mod_spec, mod_spec, full((1, d)),
            full(rw_pad.shape), full((1, LANES)),
            pl.BlockSpec((EXPERTS_PER_STEP, d, dff), lambda i, e: (lw + e, 0, 0)),
            pl.BlockSpec((EXPERTS_PER_STEP, d, dff), lambda i, e: (lw + e, 0, 0)),
            pl.BlockSpec((EXPERTS_PER_STEP, dff, d), lambda i, e: (lw + e, 0, 0)),
            full((1, d)),
        ],
        out_specs=x_spec,
        out_shape=jax.ShapeDtypeStruct((bsz, t_len, d), F32),
        scratch_shapes=[
            pltpu.VMEM((tm, d + LANES), _cdt(hi)),
            pltpu.VMEM((tm, LANES), F32),
            pltpu.VMEM((8, tm), F32),
            pltpu.VMEM((cap_rows, d), _cdt(hi)),
            pltpu.VMEM((cap_rows, LANES), F32),
            pltpu.VMEM((cap_rows, d), F32),
            pltpu.VMEM((tm, d), F32),
            pltpu.SMEM((N_EXPERTS // EPG,), jnp.int32),
        ],
        compiler_params=_params(("parallel", "arbitrary")),
        name="moe_hi" if hi else "moe",
    )(x, sc, sh, g2, norm_g.reshape(1, d), rw_pad, rb_pad, w1, w3, w2, final_g.reshape(1, d))


def _trunk(x, mods, pos0, t_valid, states, params, moe_weights, flat_moe, attn_tiles, proj_tm, moe_tm, hi):
    (norm_mix_g, norm_ffn_g, final_norm_g, w_in_even, i_bias, f_bias, mnorm_g, rnorm_g, w_out_even,
     w_in_odd, pool_w, pool_scale, lq1, lk1, lq2, lk2, dnorm_g, w_out_odd, router_w, router_b) = params
    c0, n0, m0, s0, pool_past, k_past, v_past = states
    bsz, t_pad, d = x.shape
    w1, w3, w2 = moe_weights

    def run_moe(x, layer, final):
        sc2, sh2, g2 = mods[layer][4], mods[layer][3], mods[layer][5]
        if flat_moe:
            n_rows = bsz * t_valid
            flat = lambda a: jnp.broadcast_to(a, (bsz, t_valid, d)).reshape(1, n_rows, d)
            y = _moe(x[:, :t_valid].reshape(1, n_rows, d), flat(sc2), flat(sh2), flat(g2), norm_ffn_g[layer],
                     router_w, router_b, w1, w3, w2, layer, final_norm_g, final, n_rows, hi)
            y = y.reshape(bsz, t_valid, d)
            return jnp.concatenate([y, jnp.zeros((bsz, t_pad - t_valid, d), F32)], axis=1)
        return _moe(x, sc2, sh2, g2, norm_ffn_g[layer], router_w, router_b,
                    w1, w3, w2, layer, final_norm_g, final, moe_tm, hi)

    sh1, sc1, g1 = mods[0][0], mods[0][1], mods[0][2]
    x, c_f, n_f, m_f, s_f = _even_mixer(x, sc1, sh1, g1, norm_mix_g[0], w_in_even[0], i_bias[0], f_bias[0],
                                        mnorm_g[0], rnorm_g[0], w_out_even[0], c0, n0, m0, s0, pos0, t_valid, hi)
    x = run_moe(x, 0, False)

    layer = 1
    lam_init = 0.8 - 0.6 * math.exp(-0.3 * layer)
    sh1, sc1, g1 = mods[1][0], mods[1][1], mods[1][2]
    k_rows, v_rows, kb, q0, q1, vt, yp, pool_new = _odd_proj(
        x, sc1, sh1, norm_mix_g[1], w_in_odd[0], pool_w[0], pool_scale[0], pool_past, pos0, t_valid, proj_tm, hi)
    if k_past is None:
        tk_valid = t_valid
    else:
        past_len = k_past.shape[1]
        tk_valid = past_len + t_valid
        kb = jnp.concatenate([k_past.reshape(bsz, past_len, 512).astype(kb.dtype), kb], axis=1)
        vt_past = jnp.concatenate([jnp.transpose(v_past, (0, 2, 3, 1)),
                                   jnp.ones((bsz, N_HEADS, VT_ROWS - DH, past_len), F32)], axis=2)
        vt = jnp.concatenate([vt_past.reshape(bsz, N_HEADS * VT_ROWS, past_len).astype(vt.dtype), vt], axis=2)
    lam_params = jnp.stack([lq1[0], lk1[0], lq2[0], lk2[0]])
    tq, tk = attn_tiles
    x = _attention(q0, q1, kb, vt, x, yp, g1, lam_params, dnorm_g[0], w_out_odd[0],
                   pos0, tk_valid, lam_init, tq, tk, hi)
    x = run_moe(x, 1, True)
    return x, (c_f, n_f, m_f, s_f), (pool_new[:, 1:], k_rows[:, :t_valid], v_rows[:, :t_valid])


def kernel(x_prompt, x_sample, c_prompt, c_sample, state_mlstm_c, state_mlstm_n, state_mlstm_m, state_ret, state_pool, cache_k, cache_v, ada_w, ada_b, norm_mix_g, norm_ffn_g, final_norm_g, w_in_even, mlstm_i_bias, mlstm_f_bias, mlstm_norm_g, ret_norm_g, w_out_even, w_in_odd, pool_w, pool_scale, lambda_q1, lambda_k1, lambda_q2, lambda_k2, diff_norm_g, w_out_odd, router_w, router_b, moe_w1, moe_w3, moe_w2):
    bp, tp, d = x_prompt.shape
    bs, ts, _ = x_sample.shape
    past_len = cache_k.shape[2]
    n_layers = ada_w.shape[0]

    c_all = jnp.concatenate([c_prompt, c_sample, jnp.zeros((16 - bp - bs, d), F32)], axis=0)
    mod_all = _adaln(c_all, ada_w, ada_b)

    def mods_for(lo, stop):
        return [[mod_all[l, lo:stop, None, k * d:(k + 1) * d] for k in range(6)] for l in range(n_layers)]

    params = (norm_mix_g, norm_ffn_g, final_norm_g, w_in_even, mlstm_i_bias, mlstm_f_bias,
              mlstm_norm_g.reshape(-1, 512), ret_norm_g.reshape(-1, 512), w_out_even,
              w_in_odd, pool_w, pool_scale, lambda_q1, lambda_k1, lambda_q2, lambda_k2, diff_norm_g,
              w_out_odd, router_w, router_b)

    zeros = lambda *s: jnp.zeros(s, F32)
    p_states = (zeros(bp, N_HEADS, DH, DH), zeros(bp, N_HEADS, DH), zeros(bp, N_HEADS),
                zeros(bp, N_HEADS, DH, DH), zeros(bp, POOL_PAD - 1, 512), None, None)
    flat_w = lambda w: w.reshape((-1,) + w.shape[2:])
    moe_f32 = (flat_w(moe_w1), flat_w(moe_w3), flat_w(moe_w2))
    moe_bf16 = tuple(w.astype(BF16) for w in moe_f32)
    y_p, ev_p, od_p = _trunk(x_prompt, mods_for(0, bp), 0, tp, p_states, params, moe_bf16,
                             flat_moe=False, attn_tiles=(min(1024, tp), min(512, tp)), proj_tm=min(512, tp),
                             moe_tm=min(1024, tp), hi=False)

    ts_pad = LANES
    x_s = jnp.concatenate([x_sample, zeros(bs, ts_pad - ts, d)], axis=1)
    s_states = (state_mlstm_c[0], state_mlstm_n[0], state_mlstm_m[0], state_ret[0], state_pool[0],
                cache_k[0], cache_v[0])
    y_s, ev_s, od_s = _trunk(x_s, mods_for(bp, bp + bs), past_len, ts, s_states, params,
                             moe_f32, flat_moe=True,
                             attn_tiles=(ts_pad, past_len + ts_pad), proj_tm=ts_pad, moe_tm=None, hi=True)
    y_s = y_s[:, :ts]

    return (y_p, y_s,
            ev_p[0][None], ev_p[1][None], ev_p[2][None], ev_p[3][None],
            od_p[0][None], od_p[1][None], od_p[2][None],
            ev_s[0][None], ev_s[1][None], ev_s[2][None], ev_s[3][None],
            od_s[0][None], od_s[1][None], od_s[2][None])
```

```python
import functools
import math

import jax
import jax.numpy as jnp
from jax import lax
from jax.experimental import pallas as pl
from jax.experimental.pallas import tpu as pltpu

F32 = jnp.float32
BF16 = jnp.bfloat16

EPS = 1e-6
ROPE_THETA = 10000.0
MASK_CHUNK = 64
MASK_SHIFT = 6
DH = 128
N_HEADS = 4
D_DIFF = 64
POOL_WINDOWS = (2, 4, 8, 16)
POOL_PAD = 16
N_EXPERTS = 16
EPG = 4
EVEN_CHUNKS_PER_STEP = 4
EXPERTS_PER_STEP = 2
STEPS_PER_GROUP = EPG // EXPERTS_PER_STEP
MOE_ROW_BLOCK = 256
NEG_BIG = -1e30
LANES = 128
VT_ROWS = DH + 16
LOG2E = 1.4426950408889634
VMEM_LIMIT = 52 * 1024 * 1024

_NT = (((1,), (1,)), ((), ()))
_TN = (((0,), (0,)), ((), ()))


def _mm(a, b, hi, dims=None):
    if hi:
        a, b, prec = a.astype(F32), b.astype(F32), lax.Precision.HIGHEST
    else:
        a, b, prec = a.astype(BF16), b.astype(BF16), None
    if dims is None:
        return jnp.dot(a, b, precision=prec, preferred_element_type=F32)
    return lax.dot_general(a, b, dims, precision=prec, preferred_element_type=F32)


def _cdt(hi):
    return F32 if hi else BF16


def _rms_mod(x, g, sc, sh):
    ms = jnp.mean(x * x, axis=-1, keepdims=True)
    return (x * lax.rsqrt(ms + EPS) * g) * (1.0 + sc) + sh


def _params(sem):
    return pltpu.CompilerParams(dimension_semantics=sem, vmem_limit_bytes=VMEM_LIMIT)


def _resident(shape):
    zeros = (0,) * len(shape)
    return pl.BlockSpec(shape, lambda *args: zeros, pipeline_mode=pl.Buffered(1))


def _adaln_kernel(c_ref, w_ref, b_ref, o_ref):
    c = c_ref[...]
    o_ref[0] = _mm(c * jax.nn.sigmoid(c), w_ref[0], True) + b_ref[0]


def _adaln(c_all, ada_w, ada_b):
    n_layers, d, n6 = ada_w.shape
    rows = c_all.shape[0]
    bn = 1536
    return pl.pallas_call(
        _adaln_kernel,
        grid=(n_layers, n6 // bn),
        in_specs=[
            pl.BlockSpec((rows, d), lambda l, j: (0, 0)),
            pl.BlockSpec((1, d, bn), lambda l, j: (l, 0, j)),
            pl.BlockSpec((1, 1, bn), lambda l, j: (l, 0, j)),
        ],
        out_specs=pl.BlockSpec((1, rows, bn), lambda l, j: (l, 0, j)),
        out_shape=jax.ShapeDtypeStruct((n_layers, rows, n6), F32),
        compiler_params=_params(("parallel", "parallel")),
        name="adaln",
    )(c_all, ada_w, ada_b.reshape(n_layers, 1, n6))


def _log_sigmoid(x):
    return jnp.minimum(x, 0.0) - jnp.log1p(jnp.exp(-jnp.abs(x)))


def _even_kernel(x_ref, sc_ref, sh_ref, g1_ref, ng_ref, wrow_ref, wkt_ref, wgt_ref, gb_ref,
                 cosr_ref, sinr_ref, cost_ref, sint_ref, mng_ref, rng_ref, wout_ref,
                 c0_ref, m0_ref, s0_ref,
                 xo_ref, cf_ref, mf_ref, sf_ref,
                 r_scr, kt_scr, c_scr, m_scr, s_scr, y_scr, *, lc, n_c, lv, hi):
    t = pl.program_id(1)
    n_t = pl.num_programs(1)
    mm = functools.partial(_mm, hi=hi)

    @pl.when(t == 0)
    def _():
        c_scr[...] = c0_ref[0]
        m_scr[...] = m0_ref[0]
        s_scr[...] = s0_ref[0]

    x = x_ref[0]
    h = _rms_mod(x, ng_ref[...], sc_ref[0], sh_ref[0]).astype(_cdt(hi))
    r_scr[...] = mm(h, wrow_ref[...])
    kt_scr[...] = mm(wkt_ref[...], h, dims=_NT) * (DH ** -0.5)
    gt = mm(wgt_ref[...], h, dims=_NT) + gb_ref[...]

    lane = lax.broadcasted_iota(jnp.int32, (1, lc), 1)
    valid = lane < lv
    row_i = lax.broadcasted_iota(jnp.int32, (lc, lc), 0)
    col_i = lax.broadcasted_iota(jnp.int32, (lc, lc), 1)
    tri = col_i <= row_i
    eye = col_i == row_i
    upper = (row_i <= col_i).astype(F32)
    rel = (row_i - col_i).astype(F32)
    jrow = lane.astype(F32)
    jcol = lax.broadcasted_iota(jnp.int32, (lc, 1), 0).astype(F32)
    ones_col = (lax.broadcasted_iota(jnp.int32, (lc, DH), 1) == 0).astype(_cdt(hi))

    for c in range(n_c):
        rs = slice(c * lc, (c + 1) * lc)
        ig = gt[0:N_HEADS, rs]
        lf = jnp.where(valid, _log_sigmoid(gt[N_HEADS:2 * N_HEADS, rs]), 0.0)
        b_all = _mm(lf, upper, True)
        u_all = jnp.where(valid, ig - b_all, -jnp.inf)

        for hd in range(N_HEADS):
            sl = slice(hd * DH, (hd + 1) * DH)
            q = r_scr[rs, sl].astype(_cdt(hi))
            v = r_scr[rs, 512 + hd * DH:512 + (hd + 1) * DH].astype(_cdt(hi))
            vaug = jnp.concatenate([v, ones_col], axis=1)
            kt = kt_scr[sl, rs]
            u_row = u_all[hd:hd + 1]
            b_row = b_all[hd:hd + 1]
            m_prev = m_scr[hd]
            caug = c_scr[hd]

            u_mat = jnp.where(tri, u_row, -jnp.inf)
            r_col = jnp.maximum(m_prev, jnp.max(u_mat, axis=1, keepdims=True))
            b_col = jnp.sum(jnp.where(eye, b_row, 0.0), axis=1, keepdims=True)
            s_qk = mm(q, kt)
            p = jnp.exp(u_mat - r_col) * s_qk
            sc_col = jnp.exp(m_prev - r_col)
            nd = sc_col * mm(q, caug) + mm(p, vaug)
            num = nd[:, :DH]
            den = nd[:, DH:DH + 1]
            m_t = b_col + r_col
            h_m = num / jnp.maximum(jnp.abs(den), jnp.exp(-m_t))
            ms = jnp.mean(h_m * h_m, axis=-1, keepdims=True)
            h_m = h_m * lax.rsqrt(ms + EPS) * mng_ref[:, sl]
            gate = jax.nn.sigmoid(r_scr[rs, 1024 + hd * DH:1024 + (hd + 1) * DH])
            y_scr[rs, sl] = (gate * h_m).astype(y_scr.dtype)

            u_max = jnp.max(u_row, axis=1, keepdims=True)
            w_row = jnp.exp(u_row - u_max)
            kv = mm(kt * w_row, vaug)
            b_end = b_row[:, lc - 1:lc]
            a_max = b_end + u_max
            m_new = jnp.maximum(b_end + m_prev, a_max)
            c_scr[hd] = jnp.exp(b_end + m_prev - m_new) * caug + jnp.exp(a_max - m_new) * kv
            m_scr[hd] = m_new

        cos_r = cosr_ref[rs, :]
        sin_r = sinr_ref[rs, :]
        cos_t = cost_ref[:, rs]
        sin_t = sint_ref[:, rs]
        for hd in range(N_HEADS):
            lg = math.log(1.0 - 2.0 ** (-5.0 - hd))
            sl = slice(hd * DH, (hd + 1) * DH)
            rq = r_scr[rs, 1536 + hd * DH:1536 + (hd + 1) * DH]
            rq = (rq * cos_r + pltpu.roll(rq, DH // 2, 1) * sin_r).astype(_cdt(hi))
            rv = r_scr[rs, 2048 + hd * DH:2048 + (hd + 1) * DH].astype(_cdt(hi))
            kt = kt_scr[512 + hd * DH:512 + (hd + 1) * DH, rs]
            top, bot = kt[:DH // 2], kt[DH // 2:]
            kt = jnp.concatenate([top * cos_t - bot * sin_t, top * sin_t + bot * cos_t], axis=0)
            s_prev = s_scr[hd]

            decay = jnp.where(rel >= 0, jnp.exp(lg * jnp.maximum(rel, 0.0)), 0.0)
            scores = mm(rq, kt) * decay
            intra = mm(scores, rv)
            inter = mm(rq, s_prev) * jnp.exp(lg * (jcol + 1.0))
            o = intra + inter
            ms = jnp.mean(o * o, axis=-1, keepdims=True)
            o = o * lax.rsqrt(ms + EPS) * rng_ref[:, sl]
            g_in = r_scr[rs, 2560 + hd * DH:2560 + (hd + 1) * DH]
            y_scr[rs, 512 + hd * DH:512 + (hd + 1) * DH] = (g_in * jax.nn.sigmoid(g_in) * o).astype(y_scr.dtype)

            w_end = jnp.where(valid, jnp.exp(lg * (lv - 1.0 - jrow)), 0.0)
            s_scr[hd] = math.exp(lg * lv) * s_prev + mm(kt * w_end, rv)

    xo_ref[0] = x + g1_ref[0] * mm(y_scr[...], wout_ref[...])

    @pl.when(t == n_t - 1)
    def _():
        cf_ref[0] = c_scr[...]
        mf_ref[0] = m_scr[...]
        sf_ref[0] = s_scr[...]


def _even_mixer(x, sc, sh, g1, norm_g, w_in, i_bias, f_bias, mnorm_g, rnorm_g, w_out,
                c0, n0, m0, s0, pos0, t_valid, hi):
    bsz, t_pad, d = x.shape
    lc = LANES
    n_c = max(1, min(EVEN_CHUNKS_PER_STEP, t_pad // lc))
    tm = n_c * lc
    n_t = t_pad // tm
    lv = lc if t_pad > lc else t_valid
    assert t_pad == lc or t_valid == t_pad
    wdt = _cdt(hi)
    w = w_in
    cols = lambda a, b: w[:, a:b]
    w_row = jnp.concatenate([cols(0, 512), cols(1024, 1536), cols(1536, 2048),
                             cols(2056, 2568), cols(3080, 3592), cols(3592, 4104)], axis=1).astype(wdt)
    w_kt = jnp.concatenate([cols(512, 1024), cols(2568, 3080)], axis=1).T.astype(wdt)
    w_gt = jnp.concatenate([cols(2048, 2056).T, jnp.zeros((8, d), F32)], axis=0).astype(wdt)
    gb = jnp.concatenate([i_bias, f_bias, jnp.zeros((8,), F32)]).reshape(16, 1)

    half = DH // 2
    inv = jnp.power(ROPE_THETA, -jnp.arange(half, dtype=F32) * 2.0 / DH)
    pos = pos0 + jnp.arange(t_pad, dtype=jnp.int32)
    ang = pos.astype(F32)[:, None] * inv[None, :]
    cos, sin = jnp.cos(ang), jnp.sin(ang)
    cos_r = jnp.concatenate([cos, cos], axis=1)
    sin_r = jnp.concatenate([-sin, sin], axis=1)
    cos_t, sin_t = cos.T, sin.T

    caug0 = jnp.concatenate([c0, n0[..., None], jnp.zeros(c0.shape[:-1] + (DH - 1,), F32)], axis=-1)
    m0 = m0.reshape(bsz, N_HEADS, 1, 1)

    full = lambda shape: _resident(shape)
    per_b = lambda shape: pl.BlockSpec((1,) + shape, lambda b, t: (b,) + (0,) * len(shape))
    outs = pl.pallas_call(
        functools.partial(_even_kernel, lc=lc, n_c=n_c, lv=lv, hi=hi),
        grid=(bsz, n_t),
        in_specs=[
            pl.BlockSpec((1, tm, d), lambda b, t: (b, t, 0)),
            per_b((1, d)), per_b((1, d)), per_b((1, d)),
            full((1, d)),
            full((d, 3072)), full((1024, d)), full((16, d)), full((16, 1)),
            pl.BlockSpec((tm, DH), lambda b, t: (t, 0)),
            pl.BlockSpec((tm, DH), lambda b, t: (t, 0)),
            pl.BlockSpec((half, tm), lambda b, t: (0, t)),
            pl.BlockSpec((half, tm), lambda b, t: (0, t)),
            full((1, 512)), full((1, 512)), full((d, d)),
            per_b((N_HEADS, DH, 2 * DH)), per_b((N_HEADS, 1, 1)), per_b((N_HEADS, DH, DH)),
        ],
        out_specs=[
            pl.BlockSpec((1, tm, d), lambda b, t: (b, t, 0)),
            per_b((N_HEADS, DH, 2 * DH)), per_b((N_HEADS, 1, 1)), per_b((N_HEADS, DH, DH)),
        ],
        out_shape=[
            jax.ShapeDtypeStruct((bsz, t_pad, d), F32),
            jax.ShapeDtypeStruct((bsz, N_HEADS, DH, 2 * DH), F32),
            jax.ShapeDtypeStruct((bsz, N_HEADS, 1, 1), F32),
            jax.ShapeDtypeStruct((bsz, N_HEADS, DH, DH), F32),
        ],
        scratch_shapes=[
            pltpu.VMEM((tm, 3072), F32),
            pltpu.VMEM((1024, tm), F32),
            pltpu.VMEM((N_HEADS, DH, 2 * DH), F32),
            pltpu.VMEM((N_HEADS, 1, 1), F32),
            pltpu.VMEM((N_HEADS, DH, DH), F32),
            pltpu.VMEM((tm, d), wdt),
        ],
        compiler_params=_params(("parallel", "arbitrary")),
        name="even_mixer_hi" if hi else "even_mixer",
    )(x, sc, sh, g1, norm_g.reshape(1, d), w_row, w_kt, w_gt, gb,
      cos_r, sin_r, cos_t, sin_t, mnorm_g.reshape(1, 512), rnorm_g.reshape(1, 512),
      w_out.astype(wdt), caug0, m0, s0)
    x_new, caug, m_f, s_f = outs
    return x_new, caug[..., :DH], caug[..., DH], m_f.reshape(bsz, N_HEADS), s_f


def _odd_proj_kernel(x_ref, sc_ref, sh_ref, ng_ref, w_ref, wvt_ref, cos_ref, sin_ref,
                     pw_ref, ps_ref, past_ref,
                     krow_ref, vrow_ref, kb_ref, q0_ref, q1_ref, vt_ref, yp_ref, pool_ref,
                     r_scr, ext_scr, *, tm, tv, pos0, hi):
    t = pl.program_id(1)
    mm = functools.partial(_mm, hi=hi)

    @pl.when(t == 0)
    def _():
        ext_scr[0:POOL_PAD] = past_ref[0]

    x = x_ref[0]
    h = _rms_mod(x, ng_ref[...], sc_ref[0], sh_ref[0]).astype(_cdt(hi))
    r_scr[...] = mm(h, w_ref[...])
    vt = mm(wvt_ref[...], h, dims=_NT)
    for hd in range(N_HEADS):
        vt_ref[0, hd * VT_ROWS:hd * VT_ROWS + DH, :] = vt[hd * DH:(hd + 1) * DH].astype(vt_ref.dtype)
        vt_ref[0, hd * VT_ROWS + DH:(hd + 1) * VT_ROWS, :] = jnp.ones((VT_ROWS - DH, tm), vt_ref.dtype)
        vrow_ref[0, :, hd, :] = r_scr[:, 1536 + hd * DH:1536 + (hd + 1) * DH]

    ext_scr[POOL_PAD:POOL_PAD + tm] = r_scr[:, 0:512]
    pos = (pos0 + t * tm + lax.broadcasted_iota(jnp.int32, (tm, 1), 0)).astype(F32)
    for g, w in enumerate(POOL_WINDOWS):
        sl = slice(g * LANES, (g + 1) * LANES)
        pin = r_scr[:, sl]
        win = pin
        for j in range(1, w):
            win = win + ext_scr[POOL_PAD - j:POOL_PAD - j + tm, sl]
        cnt = jnp.minimum(pos + 1.0, float(w))
        dev = win / cnt - pin
        yp_ref[0, :, sl] = (mm(dev, pw_ref[g]) * ps_ref[:, sl]).astype(yp_ref.dtype)
    pool_ref[0] = ext_scr[tv:tv + POOL_PAD]
    ext_scr[0:POOL_PAD] = ext_scr[tm:tm + POOL_PAD]

    lane = lax.broadcasted_iota(jnp.int32, (1, LANES), 1)
    low_half = (lane & (D_DIFF - 1)) < (D_DIFF // 2)
    comp0 = lane < D_DIFF
    cos = cos_ref[...]
    sin = sin_ref[...]

    def rope(a):
        swapped = jnp.where(low_half, pltpu.roll(a, LANES - D_DIFF // 2, 1), pltpu.roll(a, D_DIFF // 2, 1))
        return a * cos + swapped * sin

    for hd in range(N_HEADS):
        sl = slice(hd * LANES, (hd + 1) * LANES)
        q = rope(r_scr[:, 512 + hd * LANES:512 + (hd + 1) * LANES]) * (D_DIFF ** -0.5 * LOG2E)
        q0_ref[0, :, sl] = jnp.where(comp0, q, 0.0).astype(q0_ref.dtype)
        q1_ref[0, :, sl] = jnp.where(comp0, 0.0, q).astype(q1_ref.dtype)
        k = rope(r_scr[:, 1024 + hd * LANES:1024 + (hd + 1) * LANES])
        krow_ref[0, :, hd, :] = k
        kb_ref[0, :, sl] = k.astype(kb_ref.dtype)


def _odd_proj(x, sc, sh, norm_g, w_in, pool_w, pool_scale, pool_past, pos0, t_valid, tm, hi):
    bsz, t_pad, d = x.shape
    n_t = t_pad // tm
    tv = tm if n_t > 1 else t_valid
    assert n_t == 1 or t_valid == t_pad
    wdt = _cdt(hi)
    w_all = w_in.astype(wdt)
    w_vt = w_in[:, 1536:2048].T.astype(wdt)

    half = D_DIFF // 2
    inv = jnp.power(ROPE_THETA, -jnp.arange(half, dtype=F32) * 2.0 / D_DIFF)
    pos = pos0 + jnp.arange(t_pad, dtype=jnp.int32)
    ang = pos.astype(F32)[:, None] * inv[None, :]
    cos, sin = jnp.cos(ang), jnp.sin(ang)
    cos_t = jnp.concatenate([cos, cos, cos, cos], axis=1)
    sin_t = jnp.concatenate([-sin, sin, -sin, sin], axis=1)
    past = jnp.concatenate([jnp.zeros((bsz, 1, 512), F32), pool_past], axis=1)

    full = lambda shape: _resident(shape)
    per_b = lambda shape: pl.BlockSpec((1,) + shape, lambda b, t: (b,) + (0,) * len(shape))
    tile = lambda n: pl.BlockSpec((1, tm, n), lambda b, t: (b, t, 0))
    head_tile = pl.BlockSpec((1, tm, N_HEADS, DH), lambda b, t: (b, t, 0, 0))
    f32_rows = jax.ShapeDtypeStruct((bsz, t_pad, N_HEADS, DH), F32)
    c_rows = jax.ShapeDtypeStruct((bsz, t_pad, 512), wdt)
    return pl.pallas_call(
        functools.partial(_odd_proj_kernel, tm=tm, tv=tv, pos0=pos0, hi=hi),
        grid=(bsz, n_t),
        in_specs=[
            tile(d), per_b((1, d)), per_b((1, d)), full((1, d)),
            full((d, 2048)), full((512, d)),
            pl.BlockSpec((tm, LANES), lambda b, t: (t, 0)),
            pl.BlockSpec((tm, LANES), lambda b, t: (t, 0)),
            full((4, LANES, LANES)), full((1, 512)), per_b((POOL_PAD, 512)),
        ],
        out_specs=[
            head_tile, head_tile, tile(512), tile(512), tile(512),
            pl.BlockSpec((1, N_HEADS * VT_ROWS, tm), lambda b, t: (b, 0, t)),
            tile(512), per_b((POOL_PAD, 512)),
        ],
        out_shape=[f32_rows, f32_rows, c_rows, c_rows, c_rows,
                   jax.ShapeDtypeStruct((bsz, N_HEADS * VT_ROWS, t_pad), wdt), c_rows,
                   jax.ShapeDtypeStruct((bsz, POOL_PAD, 512), F32)],
        scratch_shapes=[pltpu.VMEM((tm, 2048), F32), pltpu.VMEM((tm + POOL_PAD, 512), F32)],
        compiler_params=_params(("parallel", "arbitrary")),
        name="odd_proj_hi" if hi else "odd_proj",
    )(x, sc, sh, norm_g.reshape(1, d), w_all, w_vt, cos_t, sin_t,
      pool_w.astype(wdt), pool_scale.reshape(1, 512), past)


def _attn_kernel(qi_ref, kj_ref, last_ref,
                 q0_ref, q1_ref, kb_ref, vt_ref, x_ref, yp_ref, g1_ref, lamp_ref, dng_ref, wout_ref,
                 xo_ref, m_scr, acc_scr, *, tq, tk, q_pos0, tk_valid, lam_init, hi):
    s = pl.program_id(1)
    qi = qi_ref[s]
    kj = kj_ref[s]
    mm = functools.partial(_mm, hi=hi)

    @pl.when(kj == 0)
    def _():
        m_scr[...] = jnp.full_like(m_scr, NEG_BIG)
        acc_scr[...] = jnp.zeros_like(acc_scr)

    q_refs = (q0_ref, q1_ref)

    def scores(i):
        sl = slice((i // 2) * LANES, (i // 2 + 1) * LANES)
        return mm(kb_ref[0, :, sl], q_refs[i % 2][0, :, sl], dims=_NT)

    def step(masked):
        if masked:
            kpos = kj * tk + lax.broadcasted_iota(jnp.int32, (tk, tq), 0)
            qpos = q_pos0 + qi * tq + lax.broadcasted_iota(jnp.int32, (tk, tq), 1)
            visible = ((kpos >> MASK_SHIFT) <= (qpos >> MASK_SHIFT)) & (kpos < tk_valid)
        st_next = scores(0)
        for i in range(2 * N_HEADS):
            st = st_next
            if i + 1 < 2 * N_HEADS:
                st_next = scores(i + 1)
            if masked:
                st = jnp.where(visible, st, NEG_BIG)
            hd = i // 2
            vt = vt_ref[0, hd * VT_ROWS:(hd + 1) * VT_ROWS, :]
            m_prev = m_scr[i:i + 1]
            m_new = jnp.maximum(m_prev, jnp.max(st, axis=0, keepdims=True))
            alpha = jnp.exp2(m_prev - m_new)
            p = jnp.exp2(st - m_new)
            acc_scr[i] = alpha * acc_scr[i] + mm(vt, p)
            m_scr[i:i + 1] = m_new

    fully_visible = ((kj + 1) * tk - 1) // MASK_CHUNK <= (q_pos0 + qi * tq) // MASK_CHUNK
    no_pad = (kj + 1) * tk <= tk_valid
    clean = jnp.logical_and(fully_visible, no_pad)
    pl.when(clean)(lambda: step(False))
    pl.when(jnp.logical_not(clean))(lambda: step(True))

    @pl.when(last_ref[s] == 1)
    def _():
        lp = lamp_ref[...]
        lam = (jnp.exp(jnp.sum(lp[0:1] * lp[1:2], axis=1, keepdims=True))
               - jnp.exp(jnp.sum(lp[2:3] * lp[3:4], axis=1, keepdims=True)) + lam_init)
        y = mm(yp_ref[0], wout_ref[0:512, :])
        for hd in range(N_HEADS):
            o0 = acc_scr[2 * hd, 0:DH] / acc_scr[2 * hd, DH:DH + 1]
            o1 = acc_scr[2 * hd + 1, 0:DH] / acc_scr[2 * hd + 1, DH:DH + 1]
            ot = o0 - lam * o1
            ms = jnp.mean(ot * ot, axis=0, keepdims=True)
            ot = (ot * lax.rsqrt(ms + EPS) * dng_ref[...]) * (1.0 - lam_init)
            y = y + mm(ot, wout_ref[512 + hd * LANES:512 + (hd + 1) * LANES, :], dims=_TN)
        xo_ref[0] = x_ref[0] + g1_ref[0] * y


def _attention(q0, q1, kb, vt, x, yp, g1, lam_params, dnorm_g, w_out, q_pos0, tk_valid, lam_init, tq, tk, hi):
    bsz, t_q, d = x.shape
    t_k = kb.shape[1]
    n_q, n_k = t_q // tq, t_k // tk
    qi, kj, last = [], [], []
    for i in range(n_q):
        q_chunk_last = (q_pos0 + (i + 1) * tq - 1) // MASK_CHUNK
        k_last = min((q_chunk_last + 1) * MASK_CHUNK, tk_valid) - 1
        j_last = min(k_last // tk, n_k - 1)
        for j in range(j_last + 1):
            qi.append(i)
            kj.append(j)
            last.append(1 if j == j_last else 0)
    qi, kj, last = (jnp.asarray(a, jnp.int32) for a in (qi, kj, last))
    n_steps = int(qi.shape[0])

    q_spec = pl.BlockSpec((1, tq, 512), lambda b, s, qi, kj, la: (b, qi[s], 0))
    full = lambda shape: _resident(shape)
    grid_spec = pltpu.PrefetchScalarGridSpec(
        num_scalar_prefetch=3,
        grid=(bsz, n_steps),
        in_specs=[
            q_spec, q_spec,
            pl.BlockSpec((1, tk, 512), lambda b, s, qi, kj, la: (b, kj[s], 0)),
            pl.BlockSpec((1, N_HEADS * VT_ROWS, tk), lambda b, s, qi, kj, la: (b, 0, kj[s])),
            pl.BlockSpec((1, tq, d), lambda b, s, qi, kj, la: (b, qi[s], 0)),
            q_spec,
            pl.BlockSpec((1, 1, d), lambda b, s, qi, kj, la: (b, 0, 0)),
            full((4, D_DIFF)), full((LANES, 1)), full((d, d)),
        ],
        out_specs=pl.BlockSpec((1, tq, d), lambda b, s, qi, kj, la: (b, qi[s], 0)),
        scratch_shapes=[
            pltpu.VMEM((2 * N_HEADS, tq), F32),
            pltpu.VMEM((2 * N_HEADS, VT_ROWS, tq), F32),
        ],
    )
    return pl.pallas_call(
        functools.partial(_attn_kernel, tq=tq, tk=tk, q_pos0=q_pos0, tk_valid=tk_valid,
                          lam_init=lam_init, hi=hi),
        grid_spec=grid_spec,
        out_shape=jax.ShapeDtypeStruct((bsz, t_q, d), F32),
        compiler_params=_params(("parallel", "arbitrary")),
        name="diff_attention_hi" if hi else "diff_attention",
    )(qi, kj, last, q0, q1, kb, vt, x, yp, g1, lam_params, dnorm_g.reshape(LANES, 1),
      w_out.astype(_cdt(hi)))


def _route(h, rw, rb, hi):
    if hi:
        logits = _mm(h, rw, True)
    else:
        h_hi = h.astype(BF16)
        h_lo = (h - h_hi.astype(F32)).astype(BF16)
        both = jnp.dot(h_hi, rw, preferred_element_type=F32)
        logits = (both[:, :LANES] + both[:, LANES:]) + jnp.dot(h_lo, rw[:, :LANES], preferred_element_type=F32)
    s = jax.nn.sigmoid(logits)
    sel = s + rb
    lane_i = lax.broadcasted_iota(jnp.int32, sel.shape, 1)
    lane = lane_i.astype(F32)
    group = (lane_i >> 2).astype(F32)

    def top2(vals):
        m1 = jnp.max(vals, axis=1, keepdims=True)
        i1 = jnp.min(jnp.where(vals == m1, lane, float(N_EXPERTS)), axis=1, keepdims=True)
        rest = jnp.where(lane == i1, -jnp.inf, vals)
        m2 = jnp.max(rest, axis=1, keepdims=True)
        i2 = jnp.min(jnp.where(rest == m2, lane, float(N_EXPERTS)), axis=1, keepdims=True)
        return m1, i1, m2, i2

    best = None
    for g in range(N_EXPERTS // EPG):
        m1, _, m2, _ = top2(jnp.where(group == float(g), sel, -jnp.inf))
        score = m1 + m2
        if best is None:
            best, gbest = score, jnp.zeros_like(score)
        else:
            better = score > best
            best = jnp.where(better, score, best)
            gbest = jnp.where(better, float(g), gbest)
    _, i1, _, i2 = top2(jnp.where(group == gbest, sel, -jnp.inf))
    w1 = jnp.sum(jnp.where(lane == i1, s, 0.0), axis=1, keepdims=True)
    w2 = jnp.sum(jnp.where(lane == i2, s, 0.0), axis=1, keepdims=True)
    tot = w1 + w2
    return jnp.where(lane == i1, w1 / tot, 0.0) + jnp.where(lane == i2, w2 / tot, 0.0), gbest


def _moe_kernel(x_ref, sc_ref, sh_ref, g2_ref, ng_ref, rw_ref, rb_ref, w1_ref, w3_ref, w2_ref, fg_ref,
                o_ref, haug_scr, key_scr, keyt_scr, xg_scr, gg_scr, yg_scr, acc_scr, cnt_smem,
                *, final_norm, hi, rb_rows):
    step = pl.program_id(1)
    grp = step // STEPS_PER_GROUP
    tm, d = x_ref.shape[1], x_ref.shape[2]
    cdt = _cdt(hi)
    mm = functools.partial(_mm, hi=hi)
    lane = lax.broadcasted_iota(jnp.int32, (1, LANES), 1)

    @pl.when(step == 0)
    def _():
        h = _rms_mod(x_ref[0], ng_ref[...], sc_ref[0], sh_ref[0])
        gates, gbest = _route(h, rw_ref[...], rb_ref[...], hi)
        g_hi = gates.astype(BF16).astype(F32)
        rem = gates - g_hi
        g_mid = rem.astype(BF16).astype(F32)
        g_lo = (rem - g_mid).astype(BF16).astype(F32)
        pieces = g_hi + pltpu.roll(g_mid, N_EXPERTS, 1) + pltpu.roll(g_lo, 2 * N_EXPERTS, 1)
        haug_scr[:, 0:d] = h.astype(cdt)
        haug_scr[:, d:d + LANES] = pieces.astype(cdt)
        member = (gbest == lane.astype(F32)).astype(BF16)
        rows_i = lax.broadcasted_iota(jnp.int32, (tm, tm), 0)
        cols_i = lax.broadcasted_iota(jnp.int32, (tm, tm), 1)
        before = (cols_i < rows_i).astype(BF16)
        rank = jnp.dot(before, member, preferred_element_type=F32)
        key = jnp.where(member > 0, rank, -1.0)
        key_scr[...] = key
        eye = (lax.broadcasted_iota(jnp.int32, (8, LANES), 0)
               == lax.broadcasted_iota(jnp.int32, (8, LANES), 1)).astype(F32)
        keyt_scr[...] = _mm(eye, key, True, dims=_NT)
        for gi in range(N_EXPERTS // EPG):
            cnt_smem[gi] = jnp.sum(jnp.where(lane == gi, member.astype(F32), 0.0)).astype(jnp.int32)
        acc_scr[...] = jnp.zeros_like(acc_scr)

    n_blk = (cnt_smem[grp] + rb_rows - 1) // rb_rows

    @pl.when(step % STEPS_PER_GROUP == 0)
    def _():
        key_row = keyt_scr[pl.ds(grp, 1), :]

        def gather(blk, carry):
            off = pl.multiple_of(blk * rb_rows, rb_rows)
            row_id = (off + lax.broadcasted_iota(jnp.int32, (rb_rows, 1), 0)).astype(F32)
            onehot = (key_row == row_id).astype(cdt)
            xa = mm(onehot, haug_scr[...])
            xg_scr[pl.ds(off, rb_rows), :] = xa[:, 0:d].astype(cdt)
            pc = xa[:, d:d + LANES]
            gg_scr[pl.ds(off, rb_rows), :] = (pc + pltpu.roll(pc, LANES - N_EXPERTS, 1)
                                              + pltpu.roll(pc, LANES - 2 * N_EXPERTS, 1))
            yg_scr[pl.ds(off, rb_rows), :] = jnp.zeros((rb_rows, d), F32)
            return carry

        lax.fori_loop(0, n_blk, gather, 0)

    def expert(blk, carry):
        off = pl.multiple_of(blk * rb_rows, rb_rows)
        xb = xg_scr[pl.ds(off, rb_rows), :]
        gates = gg_scr[pl.ds(off, rb_rows), :]
        y = yg_scr[pl.ds(off, rb_rows), :]
        for j in range(EXPERTS_PER_STEP):
            a = mm(xb, w1_ref[j])
            b = mm(xb, w3_ref[j])
            ff = mm(a * jax.nn.sigmoid(a) * b, w2_ref[j])
            e = step * EXPERTS_PER_STEP + j
            y = y + jnp.sum(jnp.where(lane == e, gates, 0.0), axis=1, keepdims=True) * ff
        yg_scr[pl.ds(off, rb_rows), :] = y
        return carry

    lax.fori_loop(0, n_blk, expert, 0)

    @pl.when(step % STEPS_PER_GROUP == STEPS_PER_GROUP - 1)
    def _():
        key_col = jnp.sum(jnp.where(lane == grp, key_scr[...], 0.0), axis=1, keepdims=True)

        def scatter(blk, carry):
            off = pl.multiple_of(blk * rb_rows, rb_rows)
            col_id = (off + lax.broadcasted_iota(jnp.int32, (1, rb_rows), 1)).astype(F32)
            onehot_t = (key_col == col_id).astype(cdt)
            acc_scr[...] += mm(onehot_t, yg_scr[pl.ds(off, rb_rows), :])
            return carry

        lax.fori_loop(0, n_blk, scatter, 0)

    @pl.when(step == N_EXPERTS // EXPERTS_PER_STEP - 1)
    def _():
        y = x_ref[0] + g2_ref[0] * acc_scr[...]
        if final_norm:
            ms = jnp.mean(y * y, axis=-1, keepdims=True)
            y = y * lax.rsqrt(ms + EPS) * fg_ref[...]
        o_ref[0] = y


def _moe(x, sc, sh, g2, norm_g, router_w, router_b, w1, w3, w2, layer, final_g, final_norm, tm, hi):
    bsz, t_len, d = x.shape
    n_t = t_len // tm
    if hi:
        rb_rows = min(tm, -(-(tm * 5 // 16) // 16) * 16)
    else:
        rb_rows = min(tm, MOE_ROW_BLOCK)
    cap_rows = -(-tm // rb_rows) * rb_rows
    rows = sc.shape[1]
    lw = layer * (N_EXPERTS // EXPERTS_PER_STEP)
    rw_pad = jnp.concatenate([router_w, jnp.zeros((d, LANES - N_EXPERTS), F32)], axis=1)
    if not hi:
        rw_hi = rw_pad.astype(BF16)
        rw_pad = jnp.concatenate([rw_hi, (rw_pad - rw_hi.astype(F32)).astype(BF16)], axis=1)
    rb_pad = jnp.concatenate([router_b, jnp.zeros((LANES - N_EXPERTS,), F32)]).reshape(1, LANES)
    if rows == 1:
        mod_spec = pl.BlockSpec((1, 1, d), lambda i, e: (i // n_t, 0, 0))
    else:
        mod_spec = pl.BlockSpec((1, tm, d), lambda i, e: (i // n_t, i % n_t, 0))
    x_spec = pl.BlockSpec((1, tm, d), lambda i, e: (i // n_t, i % n_t, 0))
    full = lambda shape: _resident(shape)
    dff = w1.shape[-1]
    return pl.pallas_call(
        functools.partial(_moe_kernel, final_norm=final_norm, hi=hi, rb_rows=rb_rows),
        grid=(bsz * n_t, N_EXPERTS // EXPERTS_PER_STEP),
        in_specs=[
            x_spec, mod_spec, mod_spec, mod_spec, full((1, d)),
            full(rw_pad.shape), full((1, LANES)),
            pl.BlockSpec((EXPERTS_PER_STEP, d, dff), lambda i, e: (lw + e, 0, 0)),
            pl.BlockSpec((EXPERTS_PER_STEP, d, dff), lambda i, e: (lw + e, 0, 0)),
            pl.BlockSpec((EXPERTS_PER_STEP, dff, d), lambda i, e: (lw + e, 0, 0)),
            full((1, d)),
        ],
        out_specs=x_spec,
        out_shape=jax.ShapeDtypeStruct((bsz, t_len, d), F32),
        scratch_shapes=[
            pltpu.VMEM((tm, d + LANES), _cdt(hi)),
            pltpu.VMEM((tm, LANES), F32),
            pltpu.VMEM((8, tm), F32),
            pltpu.VMEM((cap_rows, d), _cdt(hi)),
            pltpu.VMEM((cap_rows, LANES), F32),
            pltpu.VMEM((cap_rows, d), F32),
            pltpu.VMEM((tm, d), F32),
            pltpu.SMEM((N_EXPERTS // EPG,), jnp.int32),
        ],
        compiler_params=_params(("parallel", "arbitrary")),
        name="moe_hi" if hi else "moe",
    )(x, sc, sh, g2, norm_g.reshape(1, d), rw_pad, rb_pad, w1, w3, w2, final_g.reshape(1, d))


def _trunk(x, mods, pos0, t_valid, states, params, moe_weights, flat_moe, attn_tiles, proj_tm, moe_tm, hi):
    (norm_mix_g, norm_ffn_g, final_norm_g, w_in_even, i_bias, f_bias, mnorm_g, rnorm_g, w_out_even,
     w_in_odd, pool_w, pool_scale, lq1, lk1, lq2, lk2, dnorm_g, w_out_odd, router_w, router_b) = params
    c0, n0, m0, s0, pool_past, k_past, v_past = states
    bsz, t_pad, d = x.shape
    w1, w3, w2 = moe_weights

    def run_moe(x, layer, final):
        sc2, sh2, g2 = mods[layer][4], mods[layer][3], mods[layer][5]
        if flat_moe:
            n_rows = bsz * t_valid
            flat = lambda a: jnp.broadcast_to(a, (bsz, t_valid, d)).reshape(1, n_rows, d)
            y = _moe(x[:, :t_valid].reshape(1, n_rows, d), flat(sc2), flat(sh2), flat(g2), norm_ffn_g[layer],
                     router_w, router_b, w1, w3, w2, layer, final_norm_g, final, n_rows, hi)
            y = y.reshape(bsz, t_valid, d)
            return jnp.concatenate([y, jnp.zeros((bsz, t_pad - t_valid, d), F32)], axis=1)
        return _moe(x, sc2, sh2, g2, norm_ffn_g[layer], router_w, router_b,
                    w1, w3, w2, layer, final_norm_g, final, moe_tm, hi)

    sh1, sc1, g1 = mods[0][0], mods[0][1], mods[0][2]
    x, c_f, n_f, m_f, s_f = _even_mixer(x, sc1, sh1, g1, norm_mix_g[0], w_in_even[0], i_bias[0], f_bias[0],
                                        mnorm_g[0], rnorm_g[0], w_out_even[0], c0, n0, m0, s0, pos0, t_valid, hi)
    x = run_moe(x, 0, False)

    layer = 1
    lam_init = 0.8 - 0.6 * math.exp(-0.3 * layer)
    sh1, sc1, g1 = mods[1][0], mods[1][1], mods[1][2]
    k_rows, v_rows, kb, q0, q1, vt, yp, pool_new = _odd_proj(
        x, sc1, sh1, norm_mix_g[1], w_in_odd[0], pool_w[0], pool_scale[0], pool_past, pos0, t_valid, proj_tm, hi)
    if k_past is None:
        tk_valid = t_valid
    else:
        past_len = k_past.shape[1]
        tk_valid = past_len + t_valid
        kb = jnp.concatenate([k_past.reshape(bsz, past_len, 512).astype(kb.dtype), kb], axis=1)
        vt_past = jnp.concatenate([jnp.transpose(v_past, (0, 2, 3, 1)),
                                   jnp.ones((bsz, N_HEADS, VT_ROWS - DH, past_len), F32)], axis=2)
        vt = jnp.concatenate([vt_past.reshape(bsz, N_HEADS * VT_ROWS, past_len).astype(vt.dtype), vt], axis=2)
    lam_params = jnp.stack([lq1[0], lk1[0], lq2[0], lk2[0]])
    tq, tk = attn_tiles
    x = _attention(q0, q1, kb, vt, x, yp, g1, lam_params, dnorm_g[0], w_out_odd[0],
                   pos0, tk_valid, lam_init, tq, tk, hi)
    x = run_moe(x, 1, True)
    return x, (c_f, n_f, m_f, s_f), (pool_new[:, 1:], k_rows[:, :t_valid], v_rows[:, :t_valid])


def kernel(x_prompt, x_sample, c_prompt, c_sample, state_mlstm_c, state_mlstm_n, state_mlstm_m, state_ret, state_pool, cache_k, cache_v, ada_w, ada_b, norm_mix_g, norm_ffn_g, final_norm_g, w_in_even, mlstm_i_bias, mlstm_f_bias, mlstm_norm_g, ret_norm_g, w_out_even, w_in_odd, pool_w, pool_scale, lambda_q1, lambda_k1, lambda_q2, lambda_k2, diff_norm_g, w_out_odd, router_w, router_b, moe_w1, moe_w3, moe_w2):
    bp, tp, d = x_prompt.shape
    bs, ts, _ = x_sample.shape
    past_len = cache_k.shape[2]
    n_layers = ada_w.shape[0]

    c_all = jnp.concatenate([c_prompt, c_sample, jnp.zeros((16 - bp - bs, d), F32)], axis=0)
    mod_all = _adaln(c_all, ada_w, ada_b)

    def mods_for(lo, stop):
        return [[mod_all[l, lo:stop, None, k * d:(k + 1) * d] for k in range(6)] for l in range(n_layers)]

    params = (norm_mix_g, norm_ffn_g, final_norm_g, w_in_even, mlstm_i_bias, mlstm_f_bias,
              mlstm_norm_g.reshape(-1, 512), ret_norm_g.reshape(-1, 512), w_out_even,
              w_in_odd, pool_w, pool_scale, lambda_q1, lambda_k1, lambda_q2, lambda_k2, diff_norm_g,
              w_out_odd, router_w, router_b)

    zeros = lambda *s: jnp.zeros(s, F32)
    p_states = (zeros(bp, N_HEADS, DH, DH), zeros(bp, N_HEADS, DH), zeros(bp, N_HEADS),
                zeros(bp, N_HEADS, DH, DH), zeros(bp, POOL_PAD - 1, 512), None, None)
    flat_w = lambda w: w.reshape((-1,) + w.shape[2:])
    moe_f32 = (flat_w(moe_w1), flat_w(moe_w3), flat_w(moe_w2))
    moe_bf16 = tuple(w.astype(BF16) for w in moe_f32)
    y_p, ev_p, od_p = _trunk(x_prompt, mods_for(0, bp), 0, tp, p_states, params, moe_bf16,
                             flat_moe=False, attn_tiles=(min(1024, tp), min(1024, tp)), proj_tm=min(512, tp),
                             moe_tm=min(1024, tp), hi=False)

    ts_pad = LANES
    x_s = jnp.concatenate([x_sample, zeros(bs, ts_pad - ts, d)], axis=1)
    s_states = (state_mlstm_c[0], state_mlstm_n[0], state_mlstm_m[0], state_ret[0], state_pool[0],
                cache_k[0], cache_v[0])
    y_s, ev_s, od_s = _trunk(x_s, mods_for(bp, bp + bs), past_len, ts, s_states, params,
                             moe_f32, flat_moe=True,
                             attn_tiles=(ts_pad, past_len + ts_pad), proj_tm=ts_pad, moe_tm=None, hi=True)
    y_s = y_s[:, :ts]

    return (y_p, y_s,
            ev_p[0][None], ev_p[1][None], ev_p[2][None], ev_p[3][None],
            od_p[0][None], od_p[1][None], od_p[2][None],
            ev_s[0][None], ev_s[1][None], ev_s[2][None], ev_s[3][None],
            od_s[0][None], od_s[1][None], od_s[2][None])
```

```python
import functools
import math

import jax
import jax.numpy as jnp
from jax import lax
from jax.experimental import pallas as pl
from jax.experimental.pallas import tpu as pltpu

F32 = jnp.float32
BF16 = jnp.bfloat16

EPS = 1e-6
ROPE_THETA = 10000.0
MASK_CHUNK = 64
MASK_SHIFT = 6
DH = 128
N_HEADS = 4
D_DIFF = 64
POOL_WINDOWS = (2, 4, 8, 16)
POOL_PAD = 16
N_EXPERTS = 16
EPG = 4
EVEN_CHUNKS_PER_STEP = 4
MOE_ROW_BLOCK = 256
NEG_BIG = -1e30
LANES = 128
VT_ROWS = DH + 16
LOG2E = 1.4426950408889634
VMEM_LIMIT = 58 * 1024 * 1024

_NT = (((1,), (1,)), ((), ()))
_TN = (((0,), (0,)), ((), ()))


def _mm(a, b, hi, dims=None):
    if hi:
        a, b, prec = a.astype(F32), b.astype(F32), lax.Precision.HIGHEST
    else:
        a, b, prec = a.astype(BF16), b.astype(BF16), None
    if dims is None:
        return jnp.dot(a, b, precision=prec, preferred_element_type=F32)
    return lax.dot_general(a, b, dims, precision=prec, preferred_element_type=F32)


def _cdt(hi):
    return F32 if hi else BF16


def _rms_mod(x, g, sc, sh):
    ms = jnp.mean(x * x, axis=-1, keepdims=True)
    return (x * lax.rsqrt(ms + EPS) * g) * (1.0 + sc) + sh


def _params(sem):
    return pltpu.CompilerParams(dimension_semantics=sem, vmem_limit_bytes=VMEM_LIMIT)


def _resident(shape):
    zeros = (0,) * len(shape)
    return pl.BlockSpec(shape, lambda *args: zeros, pipeline_mode=pl.Buffered(1))


def _adaln_kernel(c_ref, w_ref, b_ref, o_ref):
    c = c_ref[...]
    o_ref[0] = _mm(c * jax.nn.sigmoid(c), w_ref[0], True) + b_ref[0]


def _adaln(c_all, ada_w, ada_b):
    n_layers, d, n6 = ada_w.shape
    rows = c_all.shape[0]
    bn = 1536
    return pl.pallas_call(
        _adaln_kernel,
        grid=(n_layers, n6 // bn),
        in_specs=[
            pl.BlockSpec((rows, d), lambda l, j: (0, 0)),
            pl.BlockSpec((1, d, bn), lambda l, j: (l, 0, j)),
            pl.BlockSpec((1, 1, bn), lambda l, j: (l, 0, j)),
        ],
        out_specs=pl.BlockSpec((1, rows, bn), lambda l, j: (l, 0, j)),
        out_shape=jax.ShapeDtypeStruct((n_layers, rows, n6), F32),
        compiler_params=_params(("parallel", "parallel")),
        name="adaln",
    )(c_all, ada_w, ada_b.reshape(n_layers, 1, n6))


def _log_sigmoid(x):
    return jnp.minimum(x, 0.0) - jnp.log1p(jnp.exp(-jnp.abs(x)))


def _even_kernel(x_ref, sc_ref, sh_ref, g1_ref, ng_ref, wrow_ref, wkt_ref, wgt_ref, gb_ref,
                 cosr_ref, sinr_ref, cost_ref, sint_ref, mng_ref, rng_ref, wout_ref,
                 c0_ref, m0_ref, s0_ref,
                 xo_ref, cf_ref, mf_ref, sf_ref,
                 r_scr, kt_scr, c_scr, m_scr, s_scr, y_scr, *, lc, n_c, lv, hi):
    t = pl.program_id(1)
    n_t = pl.num_programs(1)
    mm = functools.partial(_mm, hi=hi)

    @pl.when(t == 0)
    def _():
        c_scr[...] = c0_ref[0]
        m_scr[...] = m0_ref[0]
        s_scr[...] = s0_ref[0]

    x = x_ref[0]
    h = _rms_mod(x, ng_ref[...], sc_ref[0], sh_ref[0]).astype(_cdt(hi))
    r_scr[...] = mm(h, wrow_ref[...])
    kt_scr[...] = mm(wkt_ref[...], h, dims=_NT) * (DH ** -0.5)
    gt = mm(wgt_ref[...], h, dims=_NT) + gb_ref[...]

    lane = lax.broadcasted_iota(jnp.int32, (1, lc), 1)
    valid = lane < lv
    row_i = lax.broadcasted_iota(jnp.int32, (lc, lc), 0)
    col_i = lax.broadcasted_iota(jnp.int32, (lc, lc), 1)
    tri = col_i <= row_i
    eye = col_i == row_i
    upper = (row_i <= col_i).astype(F32)
    rel = (row_i - col_i).astype(F32)
    jrow = lane.astype(F32)
    jcol = lax.broadcasted_iota(jnp.int32, (lc, 1), 0).astype(F32)
    ones_col = (lax.broadcasted_iota(jnp.int32, (lc, DH), 1) == 0).astype(_cdt(hi))

    for c in range(n_c):
        rs = slice(c * lc, (c + 1) * lc)
        ig = gt[0:N_HEADS, rs]
        lf = jnp.where(valid, _log_sigmoid(gt[N_HEADS:2 * N_HEADS, rs]), 0.0)
        b_all = _mm(lf, upper, True)
        u_all = jnp.where(valid, ig - b_all, -jnp.inf)

        for hd in range(N_HEADS):
            sl = slice(hd * DH, (hd + 1) * DH)
            q = r_scr[rs, sl].astype(_cdt(hi))
            v = r_scr[rs, 512 + hd * DH:512 + (hd + 1) * DH].astype(_cdt(hi))
            vaug = jnp.concatenate([v, ones_col], axis=1)
            kt = kt_scr[sl, rs]
            u_row = u_all[hd:hd + 1]
            b_row = b_all[hd:hd + 1]
            m_prev = m_scr[hd]
            caug = c_scr[hd]

            u_mat = jnp.where(tri, u_row, -jnp.inf)
            r_col = jnp.maximum(m_prev, jnp.max(u_mat, axis=1, keepdims=True))
            b_col = jnp.sum(jnp.where(eye, b_row, 0.0), axis=1, keepdims=True)
            s_qk = mm(q, kt)
            p = jnp.exp(u_mat - r_col) * s_qk
            sc_col = jnp.exp(m_prev - r_col)
            nd = sc_col * mm(q, caug) + mm(p, vaug)
            num = nd[:, :DH]
            den = nd[:, DH:DH + 1]
            m_t = b_col + r_col
            h_m = num / jnp.maximum(jnp.abs(den), jnp.exp(-m_t))
            ms = jnp.mean(h_m * h_m, axis=-1, keepdims=True)
            h_m = h_m * lax.rsqrt(ms + EPS) * mng_ref[:, sl]
            gate = jax.nn.sigmoid(r_scr[rs, 1024 + hd * DH:1024 + (hd + 1) * DH])
            y_scr[rs, sl] = (gate * h_m).astype(y_scr.dtype)

            u_max = jnp.max(u_row, axis=1, keepdims=True)
            w_row = jnp.exp(u_row - u_max)
            kv = mm(kt * w_row, vaug)
            b_end = b_row[:, lc - 1:lc]
            a_max = b_end + u_max
            m_new = jnp.maximum(b_end + m_prev, a_max)
            c_scr[hd] = jnp.exp(b_end + m_prev - m_new) * caug + jnp.exp(a_max - m_new) * kv
            m_scr[hd] = m_new

        cos_r = cosr_ref[rs, :]
        sin_r = sinr_ref[rs, :]
        cos_t = cost_ref[:, rs]
        sin_t = sint_ref[:, rs]
        for hd in range(N_HEADS):
            lg = math.log(1.0 - 2.0 ** (-5.0 - hd))
            sl = slice(hd * DH, (hd + 1) * DH)
            rq = r_scr[rs, 1536 + hd * DH:1536 + (hd + 1) * DH]
            rq = (rq * cos_r + pltpu.roll(rq, DH // 2, 1) * sin_r).astype(_cdt(hi))
            rv = r_scr[rs, 2048 + hd * DH:2048 + (hd + 1) * DH].astype(_cdt(hi))
            kt = kt_scr[512 + hd * DH:512 + (hd + 1) * DH, rs]
            top, bot = kt[:DH // 2], kt[DH // 2:]
            kt = jnp.concatenate([top * cos_t - bot * sin_t, top * sin_t + bot * cos_t], axis=0)
            s_prev = s_scr[hd]

            decay = jnp.where(rel >= 0, jnp.exp(lg * jnp.maximum(rel, 0.0)), 0.0)
            scores = mm(rq, kt) * decay
            intra = mm(scores, rv)
            inter = mm(rq, s_prev) * jnp.exp(lg * (jcol + 1.0))
            o = intra + inter
            ms = jnp.mean(o * o, axis=-1, keepdims=True)
            o = o * lax.rsqrt(ms + EPS) * rng_ref[:, sl]
            g_in = r_scr[rs, 2560 + hd * DH:2560 + (hd + 1) * DH]
            y_scr[rs, 512 + hd * DH:512 + (hd + 1) * DH] = (g_in * jax.nn.sigmoid(g_in) * o).astype(y_scr.dtype)

            w_end = jnp.where(valid, jnp.exp(lg * (lv - 1.0 - jrow)), 0.0)
            s_scr[hd] = math.exp(lg * lv) * s_prev + mm(kt * w_end, rv)

    xo_ref[0] = x + g1_ref[0] * mm(y_scr[...], wout_ref[...])

    @pl.when(t == n_t - 1)
    def _():
        cf_ref[0] = c_scr[...]
        mf_ref[0] = m_scr[...]
        sf_ref[0] = s_scr[...]


def _even_mixer(x, sc, sh, g1, norm_g, w_in, i_bias, f_bias, mnorm_g, rnorm_g, w_out,
                c0, n0, m0, s0, pos0, t_valid, hi):
    bsz, t_pad, d = x.shape
    lc = LANES
    n_c = max(1, min(EVEN_CHUNKS_PER_STEP, t_pad // lc))
    tm = n_c * lc
    n_t = t_pad // tm
    lv = lc if t_pad > lc else t_valid
    assert t_pad == lc or t_valid == t_pad
    wdt = _cdt(hi)
    w = w_in
    cols = lambda a, b: w[:, a:b]
    w_row = jnp.concatenate([cols(0, 512), cols(1024, 1536), cols(1536, 2048),
                             cols(2056, 2568), cols(3080, 3592), cols(3592, 4104)], axis=1).astype(wdt)
    w_kt = jnp.concatenate([cols(512, 1024), cols(2568, 3080)], axis=1).T.astype(wdt)
    w_gt = jnp.concatenate([cols(2048, 2056).T, jnp.zeros((8, d), F32)], axis=0).astype(wdt)
    gb = jnp.concatenate([i_bias, f_bias, jnp.zeros((8,), F32)]).reshape(16, 1)

    half = DH // 2
    inv = jnp.power(ROPE_THETA, -jnp.arange(half, dtype=F32) * 2.0 / DH)
    pos = pos0 + jnp.arange(t_pad, dtype=jnp.int32)
    ang = pos.astype(F32)[:, None] * inv[None, :]
    cos, sin = jnp.cos(ang), jnp.sin(ang)
    cos_r = jnp.concatenate([cos, cos], axis=1)
    sin_r = jnp.concatenate([-sin, sin], axis=1)
    cos_t, sin_t = cos.T, sin.T

    caug0 = jnp.concatenate([c0, n0[..., None], jnp.zeros(c0.shape[:-1] + (DH - 1,), F32)], axis=-1)
    m0 = m0.reshape(bsz, N_HEADS, 1, 1)

    full = lambda shape: _resident(shape)
    per_b = lambda shape: pl.BlockSpec((1,) + shape, lambda b, t: (b,) + (0,) * len(shape))
    outs = pl.pallas_call(
        functools.partial(_even_kernel, lc=lc, n_c=n_c, lv=lv, hi=hi),
        grid=(bsz, n_t),
        in_specs=[
            pl.BlockSpec((1, tm, d), lambda b, t: (b, t, 0)),
            per_b((1, d)), per_b((1, d)), per_b((1, d)),
            full((1, d)),
            full((d, 3072)), full((1024, d)), full((16, d)), full((16, 1)),
            pl.BlockSpec((tm, DH), lambda b, t: (t, 0)),
            pl.BlockSpec((tm, DH), lambda b, t: (t, 0)),
            pl.BlockSpec((half, tm), lambda b, t: (0, t)),
            pl.BlockSpec((half, tm), lambda b, t: (0, t)),
            full((1, 512)), full((1, 512)), full((d, d)),
            per_b((N_HEADS, DH, 2 * DH)), per_b((N_HEADS, 1, 1)), per_b((N_HEADS, DH, DH)),
        ],
        out_specs=[
            pl.BlockSpec((1, tm, d), lambda b, t: (b, t, 0)),
            per_b((N_HEADS, DH, 2 * DH)), per_b((N_HEADS, 1, 1)), per_b((N_HEADS, DH, DH)),
        ],
        out_shape=[
            jax.ShapeDtypeStruct((bsz, t_pad, d), F32),
            jax.ShapeDtypeStruct((bsz, N_HEADS, DH, 2 * DH), F32),
            jax.ShapeDtypeStruct((bsz, N_HEADS, 1, 1), F32),
            jax.ShapeDtypeStruct((bsz, N_HEADS, DH, DH), F32),
        ],
        scratch_shapes=[
            pltpu.VMEM((tm, 3072), F32),
            pltpu.VMEM((1024, tm), F32),
            pltpu.VMEM((N_HEADS, DH, 2 * DH), F32),
            pltpu.VMEM((N_HEADS, 1, 1), F32),
            pltpu.VMEM((N_HEADS, DH, DH), F32),
            pltpu.VMEM((tm, d), wdt),
        ],
        compiler_params=_params(("parallel", "arbitrary")),
        name="even_mixer_hi" if hi else "even_mixer",
    )(x, sc, sh, g1, norm_g.reshape(1, d), w_row, w_kt, w_gt, gb,
      cos_r, sin_r, cos_t, sin_t, mnorm_g.reshape(1, 512), rnorm_g.reshape(1, 512),
      w_out.astype(wdt), caug0, m0, s0)
    x_new, caug, m_f, s_f = outs
    return x_new, caug[..., :DH], caug[..., DH], m_f.reshape(bsz, N_HEADS), s_f


def _odd_proj_kernel(x_ref, sc_ref, sh_ref, ng_ref, w_ref, wvt_ref, cos_ref, sin_ref,
                     pw_ref, ps_ref, past_ref,
                     krow_ref, vrow_ref, kb_ref, q0_ref, q1_ref, vt_ref, yp_ref, pool_ref,
                     r_scr, ext_scr, *, tm, tv, pos0, hi):
    t = pl.program_id(1)
    mm = functools.partial(_mm, hi=hi)

    @pl.when(t == 0)
    def _():
        ext_scr[0:POOL_PAD] = past_ref[0]

    x = x_ref[0]
    h = _rms_mod(x, ng_ref[...], sc_ref[0], sh_ref[0]).astype(_cdt(hi))
    r_scr[...] = mm(h, w_ref[...])
    vt = mm(wvt_ref[...], h, dims=_NT)
    for hd in range(N_HEADS):
        vt_ref[0, hd * VT_ROWS:hd * VT_ROWS + DH, :] = vt[hd * DH:(hd + 1) * DH].astype(vt_ref.dtype)
        vt_ref[0, hd * VT_ROWS + DH:(hd + 1) * VT_ROWS, :] = jnp.ones((VT_ROWS - DH, tm), vt_ref.dtype)
        vrow_ref[0, :, hd, :] = r_scr[:, 1536 + hd * DH:1536 + (hd + 1) * DH]

    ext_scr[POOL_PAD:POOL_PAD + tm] = r_scr[:, 0:512]
    pos = (pos0 + t * tm + lax.broadcasted_iota(jnp.int32, (tm, 1), 0)).astype(F32)
    for g, w in enumerate(POOL_WINDOWS):
        sl = slice(g * LANES, (g + 1) * LANES)
        pin = r_scr[:, sl]
        win = pin
        for j in range(1, w):
            win = win + ext_scr[POOL_PAD - j:POOL_PAD - j + tm, sl]
        cnt = jnp.minimum(pos + 1.0, float(w))
        dev = win / cnt - pin
        yp_ref[0, :, sl] = (mm(dev, pw_ref[g]) * ps_ref[:, sl]).astype(yp_ref.dtype)
    pool_ref[0] = ext_scr[tv:tv + POOL_PAD]
    ext_scr[0:POOL_PAD] = ext_scr[tm:tm + POOL_PAD]

    lane = lax.broadcasted_iota(jnp.int32, (1, LANES), 1)
    low_half = (lane & (D_DIFF - 1)) < (D_DIFF // 2)
    comp0 = lane < D_DIFF
    cos = cos_ref[...]
    sin = sin_ref[...]

    def rope(a):
        swapped = jnp.where(low_half, pltpu.roll(a, LANES - D_DIFF // 2, 1), pltpu.roll(a, D_DIFF // 2, 1))
        return a * cos + swapped * sin

    for hd in range(N_HEADS):
        sl = slice(hd * LANES, (hd + 1) * LANES)
        q = rope(r_scr[:, 512 + hd * LANES:512 + (hd + 1) * LANES]) * (D_DIFF ** -0.5 * LOG2E)
        q0_ref[0, :, sl] = jnp.where(comp0, q, 0.0).astype(q0_ref.dtype)
        q1_ref[0, :, sl] = jnp.where(comp0, 0.0, q).astype(q1_ref.dtype)
        k = rope(r_scr[:, 1024 + hd * LANES:1024 + (hd + 1) * LANES])
        krow_ref[0, :, hd, :] = k
        kb_ref[0, :, sl] = k.astype(kb_ref.dtype)


def _odd_proj(x, sc, sh, norm_g, w_in, pool_w, pool_scale, pool_past, pos0, t_valid, tm, hi):
    bsz, t_pad, d = x.shape
    n_t = t_pad // tm
    tv = tm if n_t > 1 else t_valid
    assert n_t == 1 or t_valid == t_pad
    wdt = _cdt(hi)
    w_all = w_in.astype(wdt)
    w_vt = w_in[:, 1536:2048].T.astype(wdt)

    half = D_DIFF // 2
    inv = jnp.power(ROPE_THETA, -jnp.arange(half, dtype=F32) * 2.0 / D_DIFF)
    pos = pos0 + jnp.arange(t_pad, dtype=jnp.int32)
    ang = pos.astype(F32)[:, None] * inv[None, :]
    cos, sin = jnp.cos(ang), jnp.sin(ang)
    cos_t = jnp.concatenate([cos, cos, cos, cos], axis=1)
    sin_t = jnp.concatenate([-sin, sin, -sin, sin], axis=1)
    past = jnp.concatenate([jnp.zeros((bsz, 1, 512), F32), pool_past], axis=1)

    full = lambda shape: _resident(shape)
    per_b = lambda shape: pl.BlockSpec((1,) + shape, lambda b, t: (b,) + (0,) * len(shape))
    tile = lambda n: pl.BlockSpec((1, tm, n), lambda b, t: (b, t, 0))
    head_tile = pl.BlockSpec((1, tm, N_HEADS, DH), lambda b, t: (b, t, 0, 0))
    f32_rows = jax.ShapeDtypeStruct((bsz, t_pad, N_HEADS, DH), F32)
    c_rows = jax.ShapeDtypeStruct((bsz, t_pad, 512), wdt)
    return pl.pallas_call(
        functools.partial(_odd_proj_kernel, tm=tm, tv=tv, pos0=pos0, hi=hi),
        grid=(bsz, n_t),
        in_specs=[
            tile(d), per_b((1, d)), per_b((1, d)), full((1, d)),
            full((d, 2048)), full((512, d)),
            pl.BlockSpec((tm, LANES), lambda b, t: (t, 0)),
            pl.BlockSpec((tm, LANES), lambda b, t: (t, 0)),
            full((4, LANES, LANES)), full((1, 512)), per_b((POOL_PAD, 512)),
        ],
        out_specs=[
            head_tile, head_tile, tile(512), tile(512), tile(512),
            pl.BlockSpec((1, N_HEADS * VT_ROWS, tm), lambda b, t: (b, 0, t)),
            tile(512), per_b((POOL_PAD, 512)),
        ],
        out_shape=[f32_rows, f32_rows, c_rows, c_rows, c_rows,
                   jax.ShapeDtypeStruct((bsz, N_HEADS * VT_ROWS, t_pad), wdt), c_rows,
                   jax.ShapeDtypeStruct((bsz, POOL_PAD, 512), F32)],
        scratch_shapes=[pltpu.VMEM((tm, 2048), F32), pltpu.VMEM((tm + POOL_PAD, 512), F32)],
        compiler_params=_params(("parallel", "arbitrary")),
        name="odd_proj_hi" if hi else "odd_proj",
    )(x, sc, sh, norm_g.reshape(1, d), w_all, w_vt, cos_t, sin_t,
      pool_w.astype(wdt), pool_scale.reshape(1, 512), past)


def _attn_kernel(qi_ref, kj_ref, last_ref,
                 q0_ref, q1_ref, kb_ref, vt_ref, x_ref, yp_ref, g1_ref, lamp_ref, dng_ref, wout_ref,
                 xo_ref, m_scr, acc_scr, *, tq, tk, q_pos0, tk_valid, lam_init, hi):
    s = pl.program_id(1)
    qi = qi_ref[s]
    kj = kj_ref[s]
    mm = functools.partial(_mm, hi=hi)

    @pl.when(kj == 0)
    def _():
        m_scr[...] = jnp.full_like(m_scr, NEG_BIG)
        acc_scr[...] = jnp.zeros_like(acc_scr)

    q_refs = (q0_ref, q1_ref)

    def scores(i):
        sl = slice((i // 2) * LANES, (i // 2 + 1) * LANES)
        return mm(kb_ref[0, :, sl], q_refs[i % 2][0, :, sl], dims=_NT)

    def step(masked):
        if masked:
            kpos = kj * tk + lax.broadcasted_iota(jnp.int32, (tk, tq), 0)
            qpos = q_pos0 + qi * tq + lax.broadcasted_iota(jnp.int32, (tk, tq), 1)
            visible = ((kpos >> MASK_SHIFT) <= (qpos >> MASK_SHIFT)) & (kpos < tk_valid)
        st_next = scores(0)
        for i in range(2 * N_HEADS):
            st = st_next
            if i + 1 < 2 * N_HEADS:
                st_next = scores(i + 1)
            if masked:
                st = jnp.where(visible, st, NEG_BIG)
            hd = i // 2
            vt = vt_ref[0, hd * VT_ROWS:(hd + 1) * VT_ROWS, :]
            m_prev = m_scr[i:i + 1]
            m_new = jnp.maximum(m_prev, jnp.max(st, axis=0, keepdims=True))
            alpha = jnp.exp2(m_prev - m_new)
            p = jnp.exp2(st - m_new)
            acc_scr[i] = alpha * acc_scr[i] + mm(vt, p)
            m_scr[i:i + 1] = m_new

    fully_visible = ((kj + 1) * tk - 1) // MASK_CHUNK <= (q_pos0 + qi * tq) // MASK_CHUNK
    no_pad = (kj + 1) * tk <= tk_valid
    clean = jnp.logical_and(fully_visible, no_pad)
    pl.when(clean)(lambda: step(False))
    pl.when(jnp.logical_not(clean))(lambda: step(True))

    @pl.when(last_ref[s] == 1)
    def _():
        lp = lamp_ref[...]
        lam = (jnp.exp(jnp.sum(lp[0:1] * lp[1:2], axis=1, keepdims=True))
               - jnp.exp(jnp.sum(lp[2:3] * lp[3:4], axis=1, keepdims=True)) + lam_init)
        y = mm(yp_ref[0], wout_ref[0:512, :])
        for hd in range(N_HEADS):
            o0 = acc_scr[2 * hd, 0:DH] / acc_scr[2 * hd, DH:DH + 1]
            o1 = acc_scr[2 * hd + 1, 0:DH] / acc_scr[2 * hd + 1, DH:DH + 1]
            ot = o0 - lam * o1
            ms = jnp.mean(ot * ot, axis=0, keepdims=True)
            ot = (ot * lax.rsqrt(ms + EPS) * dng_ref[...]) * (1.0 - lam_init)
            y = y + mm(ot, wout_ref[512 + hd * LANES:512 + (hd + 1) * LANES, :], dims=_TN)
        xo_ref[0] = x_ref[0] + g1_ref[0] * y


def _attention(q0, q1, kb, vt, x, yp, g1, lam_params, dnorm_g, w_out, q_pos0, tk_valid, lam_init, tq, tk, hi):
    bsz, t_q, d = x.shape
    t_k = kb.shape[1]
    n_q, n_k = t_q // tq, t_k // tk
    qi, kj, last = [], [], []
    for i in range(n_q):
        q_chunk_last = (q_pos0 + (i + 1) * tq - 1) // MASK_CHUNK
        k_last = min((q_chunk_last + 1) * MASK_CHUNK, tk_valid) - 1
        j_last = min(k_last // tk, n_k - 1)
        for j in range(j_last + 1):
            qi.append(i)
            kj.append(j)
            last.append(1 if j == j_last else 0)
    qi, kj, last = (jnp.asarray(a, jnp.int32) for a in (qi, kj, last))
    n_steps = int(qi.shape[0])

    q_spec = pl.BlockSpec((1, tq, 512), lambda b, s, qi, kj, la: (b, qi[s], 0))
    full = lambda shape: _resident(shape)
    grid_spec = pltpu.PrefetchScalarGridSpec(
        num_scalar_prefetch=3,
        grid=(bsz, n_steps),
        in_specs=[
            q_spec, q_spec,
            pl.BlockSpec((1, tk, 512), lambda b, s, qi, kj, la: (b, kj[s], 0)),
            pl.BlockSpec((1, N_HEADS * VT_ROWS, tk), lambda b, s, qi, kj, la: (b, 0, kj[s])),
            pl.BlockSpec((1, tq, d), lambda b, s, qi, kj, la: (b, qi[s], 0)),
            q_spec,
            pl.BlockSpec((1, 1, d), lambda b, s, qi, kj, la: (b, 0, 0)),
            full((4, D_DIFF)), full((LANES, 1)), full((d, d)),
        ],
        out_specs=pl.BlockSpec((1, tq, d), lambda b, s, qi, kj, la: (b, qi[s], 0)),
        scratch_shapes=[
            pltpu.VMEM((2 * N_HEADS, tq), F32),
            pltpu.VMEM((2 * N_HEADS, VT_ROWS, tq), F32),
        ],
    )
    return pl.pallas_call(
        functools.partial(_attn_kernel, tq=tq, tk=tk, q_pos0=q_pos0, tk_valid=tk_valid,
                          lam_init=lam_init, hi=hi),
        grid_spec=grid_spec,
        out_shape=jax.ShapeDtypeStruct((bsz, t_q, d), F32),
        compiler_params=_params(("parallel", "arbitrary")),
        name="diff_attention_hi" if hi else "diff_attention",
    )(qi, kj, last, q0, q1, kb, vt, x, yp, g1, lam_params, dnorm_g.reshape(LANES, 1),
      w_out.astype(_cdt(hi)))


def _route(h, rw, rb, hi):
    if hi:
        logits = _mm(h, rw, True)
    else:
        h_hi = h.astype(BF16)
        h_lo = (h - h_hi.astype(F32)).astype(BF16)
        both = jnp.dot(h_hi, rw, preferred_element_type=F32)
        logits = (both[:, :LANES] + both[:, LANES:]) + jnp.dot(h_lo, rw[:, :LANES], preferred_element_type=F32)
    s = jax.nn.sigmoid(logits)
    sel = s + rb
    lane_i = lax.broadcasted_iota(jnp.int32, sel.shape, 1)
    lane = lane_i.astype(F32)
    group = (lane_i >> 2).astype(F32)

    def top2(vals):
        m1 = jnp.max(vals, axis=1, keepdims=True)
        i1 = jnp.min(jnp.where(vals == m1, lane, float(N_EXPERTS)), axis=1, keepdims=True)
        rest = jnp.where(lane == i1, -jnp.inf, vals)
        m2 = jnp.max(rest, axis=1, keepdims=True)
        i2 = jnp.min(jnp.where(rest == m2, lane, float(N_EXPERTS)), axis=1, keepdims=True)
        return m1, i1, m2, i2

    best = None
    for g in range(N_EXPERTS // EPG):
        m1, _, m2, _ = top2(jnp.where(group == float(g), sel, -jnp.inf))
        score = m1 + m2
        if best is None:
            best, gbest = score, jnp.zeros_like(score)
        else:
            better = score > best
            best = jnp.where(better, score, best)
            gbest = jnp.where(better, float(g), gbest)
    _, i1, _, i2 = top2(jnp.where(group == gbest, sel, -jnp.inf))
    w1 = jnp.sum(jnp.where(lane == i1, s, 0.0), axis=1, keepdims=True)
    w2 = jnp.sum(jnp.where(lane == i2, s, 0.0), axis=1, keepdims=True)
    tot = w1 + w2
    return jnp.where(lane == i1, w1 / tot, 0.0) + jnp.where(lane == i2, w2 / tot, 0.0), gbest


def _moe_kernel(x_ref, sc_ref, sh_ref, g2_ref, ng_ref, rw_ref, rb_ref, w1_ref, w3_ref, w2_ref, fg_ref,
                o_ref, haug_scr, key_scr, keyt_scr, xg_scr, gg_scr, yg_scr, acc_scr, cnt_smem,
                *, final_norm, hi, rb_rows):
    EXPERTS_PER_STEP = w1_ref.shape[0]
    STEPS_PER_GROUP = EPG // EXPERTS_PER_STEP
    step = pl.program_id(1)
    grp = step // STEPS_PER_GROUP
    tm, d = x_ref.shape[1], x_ref.shape[2]
    cdt = _cdt(hi)
    mm = functools.partial(_mm, hi=hi)
    lane = lax.broadcasted_iota(jnp.int32, (1, LANES), 1)

    @pl.when(step == 0)
    def _():
        h = _rms_mod(x_ref[0], ng_ref[...], sc_ref[0], sh_ref[0])
        gates, gbest = _route(h, rw_ref[...], rb_ref[...], hi)
        g_hi = gates.astype(BF16).astype(F32)
        rem = gates - g_hi
        g_mid = rem.astype(BF16).astype(F32)
        g_lo = (rem - g_mid).astype(BF16).astype(F32)
        pieces = g_hi + pltpu.roll(g_mid, N_EXPERTS, 1) + pltpu.roll(g_lo, 2 * N_EXPERTS, 1)
        haug_scr[:, 0:d] = h.astype(cdt)
        haug_scr[:, d:d + LANES] = pieces.astype(cdt)
        member = (gbest == lane.astype(F32)).astype(BF16)
        rows_i = lax.broadcasted_iota(jnp.int32, (tm, tm), 0)
        cols_i = lax.broadcasted_iota(jnp.int32, (tm, tm), 1)
        before = (cols_i < rows_i).astype(BF16)
        rank = jnp.dot(before, member, preferred_element_type=F32)
        key = jnp.where(member > 0, rank, -1.0)
        key_scr[...] = key
        eye = (lax.broadcasted_iota(jnp.int32, (8, LANES), 0)
               == lax.broadcasted_iota(jnp.int32, (8, LANES), 1)).astype(F32)
        keyt_scr[...] = _mm(eye, key, True, dims=_NT)
        for gi in range(N_EXPERTS // EPG):
            cnt_smem[gi] = jnp.sum(jnp.where(lane == gi, member.astype(F32), 0.0)).astype(jnp.int32)
        acc_scr[...] = jnp.zeros_like(acc_scr)

    n_blk = (cnt_smem[grp] + rb_rows - 1) // rb_rows

    @pl.when(step % STEPS_PER_GROUP == 0)
    def _():
        key_row = keyt_scr[pl.ds(grp, 1), :]

        def gather(blk, carry):
            off = pl.multiple_of(blk * rb_rows, rb_rows)
            row_id = (off + lax.broadcasted_iota(jnp.int32, (rb_rows, 1), 0)).astype(F32)
            onehot = (key_row == row_id).astype(cdt)
            xa = mm(onehot, haug_scr[...])
            xg_scr[pl.ds(off, rb_rows), :] = xa[:, 0:d].astype(cdt)
            pc = xa[:, d:d + LANES]
            gg_scr[pl.ds(off, rb_rows), :] = (pc + pltpu.roll(pc, LANES - N_EXPERTS, 1)
                                              + pltpu.roll(pc, LANES - 2 * N_EXPERTS, 1))
            yg_scr[pl.ds(off, rb_rows), :] = jnp.zeros((rb_rows, d), F32)
            return carry

        lax.fori_loop(0, n_blk, gather, 0)

    def expert(blk, carry):
        off = pl.multiple_of(blk * rb_rows, rb_rows)
        xb = xg_scr[pl.ds(off, rb_rows), :]
        gates = gg_scr[pl.ds(off, rb_rows), :]
        y = yg_scr[pl.ds(off, rb_rows), :]
        for j in range(EXPERTS_PER_STEP):
            a = mm(xb, w1_ref[j])
            b = mm(xb, w3_ref[j])
            ff = mm(a * jax.nn.sigmoid(a) * b, w2_ref[j])
            e = step * EXPERTS_PER_STEP + j
            y = y + jnp.sum(jnp.where(lane == e, gates, 0.0), axis=1, keepdims=True) * ff
        yg_scr[pl.ds(off, rb_rows), :] = y
        return carry

    lax.fori_loop(0, n_blk, expert, 0)

    @pl.when(step % STEPS_PER_GROUP == STEPS_PER_GROUP - 1)
    def _():
        key_col = jnp.sum(jnp.where(lane == grp, key_scr[...], 0.0), axis=1, keepdims=True)

        def scatter(blk, carry):
            off = pl.multiple_of(blk * rb_rows, rb_rows)
            col_id = (off + lax.broadcasted_iota(jnp.int32, (1, rb_rows), 1)).astype(F32)
            onehot_t = (key_col == col_id).astype(cdt)
            acc_scr[...] += mm(onehot_t, yg_scr[pl.ds(off, rb_rows), :])
            return carry

        lax.fori_loop(0, n_blk, scatter, 0)

    @pl.when(step == N_EXPERTS // EXPERTS_PER_STEP - 1)
    def _():
        y = x_ref[0] + g2_ref[0] * acc_scr[...]
        if final_norm:
            ms = jnp.mean(y * y, axis=-1, keepdims=True)
            y = y * lax.rsqrt(ms + EPS) * fg_ref[...]
        o_ref[0] = y


def _moe(x, sc, sh, g2, norm_g, router_w, router_b, w1, w3, w2, layer, final_g, final_norm, tm, hi):
    bsz, t_len, d = x.shape
    n_t = t_len // tm
    if hi:
        rb_rows = min(tm, -(-(tm * 5 // 16) // 16) * 16)
    else:
        rb_rows = min(tm, MOE_ROW_BLOCK)
    cap_rows = -(-tm // rb_rows) * rb_rows
    rows = sc.shape[1]
    EXPERTS_PER_STEP = EPG // 2 if hi else EPG
    lw = layer * (N_EXPERTS // EXPERTS_PER_STEP)
    rw_pad = jnp.concatenate([router_w, jnp.zeros((d, LANES - N_EXPERTS), F32)], axis=1)
    if not hi:
        rw_hi = rw_pad.astype(BF16)
        rw_pad = jnp.concatenate([rw_hi, (rw_pad - rw_hi.astype(F32)).astype(BF16)], axis=1)
    rb_pad = jnp.concatenate([router_b, jnp.zeros((LANES - N_EXPERTS,), F32)]).reshape(1, LANES)
    if rows == 1:
        mod_spec = pl.BlockSpec((1, 1, d), lambda i, e: (i // n_t, 0, 0))
    else:
        mod_spec = pl.BlockSpec((1, tm, d), lambda i, e: (i // n_t, i % n_t, 0))
    x_spec = pl.BlockSpec((1, tm, d), lambda i, e: (i // n_t, i % n_t, 0))
    full = lambda shape: _resident(shape)
    dff = w1.shape[-1]
    return pl.pallas_call(
        functools.partial(_moe_kernel, final_norm=final_norm, hi=hi, rb_rows=rb_rows),
        grid=(bsz * n_t, N_EXPERTS // EXPERTS_PER_STEP),
        in_specs=[
            x_spec, mod_spec, mod_spec, mod_spec, full((1, d)),
            full(rw_pad.shape), full((1, LANES)),
            pl.BlockSpec((EXPERTS_PER_STEP, d, dff), lambda i, e: (lw + e, 0, 0)),
            pl.BlockSpec((EXPERTS_PER_STEP, d, dff), lambda i, e: (lw + e, 0, 0)),
            pl.BlockSpec((EXPERTS_PER_STEP, dff, d), lambda i, e: (lw + e, 0, 0)),
            full((1, d)),
        ],
        out_specs=x_spec,
        out_shape=jax.ShapeDtypeStruct((bsz, t_len, d), F32),
        scratch_shapes=[
            pltpu.VMEM((tm, d + LANES), _cdt(hi)),
            pltpu.VMEM((tm, LANES), F32),
            pltpu.VMEM((8, tm), F32),
            pltpu.VMEM((cap_rows, d), _cdt(hi)),
            pltpu.VMEM((cap_rows, LANES), F32),
            pltpu.VMEM((cap_rows, d), F32),
            pltpu.VMEM((tm, d), F32),
            pltpu.SMEM((N_EXPERTS // EPG,), jnp.int32),
        ],
        compiler_params=_params(("parallel", "arbitrary")),
        name="moe_hi" if hi else "moe",
    )(x, sc, sh, g2, norm_g.reshape(1, d), rw_pad, rb_pad, w1, w3, w2, final_g.reshape(1, d))


def _trunk(x, mods, pos0, t_valid, states, params, moe_weights, flat_moe, attn_tiles, proj_tm, moe_tm, hi):
    (norm_mix_g, norm_ffn_g, final_norm_g, w_in_even, i_bias, f_bias, mnorm_g, rnorm_g, w_out_even,
     w_in_odd, pool_w, pool_scale, lq1, lk1, lq2, lk2, dnorm_g, w_out_odd, router_w, router_b) = params
    c0, n0, m0, s0, pool_past, k_past, v_past = states
    bsz, t_pad, d = x.shape
    w1, w3, w2 = moe_weights

    def run_moe(x, layer, final):
        sc2, sh2, g2 = mods[layer][4], mods[layer][3], mods[layer][5]
        if flat_moe:
            n_rows = bsz * t_valid
            flat = lambda a: jnp.broadcast_to(a, (bsz, t_valid, d)).reshape(1, n_rows, d)
            y = _moe(x[:, :t_valid].reshape(1, n_rows, d), flat(sc2), flat(sh2), flat(g2), norm_ffn_g[layer],
                     router_w, router_b, w1, w3, w2, layer, final_norm_g, final, n_rows, hi)
            y = y.reshape(bsz, t_valid, d)
            return jnp.concatenate([y, jnp.zeros((bsz, t_pad - t_valid, d), F32)], axis=1)
        return _moe(x, sc2, sh2, g2, norm_ffn_g[layer], router_w, router_b,
                    w1, w3, w2, layer, final_norm_g, final, moe_tm, hi)

    sh1, sc1, g1 = mods[0][0], mods[0][1], mods[0][2]
    x, c_f, n_f, m_f, s_f = _even_mixer(x, sc1, sh1, g1, norm_mix_g[0], w_in_even[0], i_bias[0], f_bias[0],
                                        mnorm_g[0], rnorm_g[0], w_out_even[0], c0, n0, m0, s0, pos0, t_valid, hi)
    x = run_moe(x, 0, False)

    layer = 1
    lam_init = 0.8 - 0.6 * math.exp(-0.3 * layer)
    sh1, sc1, g1 = mods[1][0], mods[1][1], mods[1][2]
    k_rows, v_rows, kb, q0, q1, vt, yp, pool_new = _odd_proj(
        x, sc1, sh1, norm_mix_g[1], w_in_odd[0], pool_w[0], pool_scale[0], pool_past, pos0, t_valid, proj_tm, hi)
    if k_past is None:
        tk_valid = t_valid
    else:
        past_len = k_past.shape[1]
        tk_valid = past_len + t_valid
        kb = jnp.concatenate([k_past.reshape(bsz, past_len, 512).astype(kb.dtype), kb], axis=1)
        vt_past = jnp.concatenate([jnp.transpose(v_past, (0, 2, 3, 1)),
                                   jnp.ones((bsz, N_HEADS, VT_ROWS - DH, past_len), F32)], axis=2)
        vt = jnp.concatenate([vt_past.reshape(bsz, N_HEADS * VT_ROWS, past_len).astype(vt.dtype), vt], axis=2)
    lam_params = jnp.stack([lq1[0], lk1[0], lq2[0], lk2[0]])
    tq, tk = attn_tiles
    x = _attention(q0, q1, kb, vt, x, yp, g1, lam_params, dnorm_g[0], w_out_odd[0],
                   pos0, tk_valid, lam_init, tq, tk, hi)
    x = run_moe(x, 1, True)
    return x, (c_f, n_f, m_f, s_f), (pool_new[:, 1:], k_rows[:, :t_valid], v_rows[:, :t_valid])


def kernel(x_prompt, x_sample, c_prompt, c_sample, state_mlstm_c, state_mlstm_n, state_mlstm_m, state_ret, state_pool, cache_k, cache_v, ada_w, ada_b, norm_mix_g, norm_ffn_g, final_norm_g, w_in_even, mlstm_i_bias, mlstm_f_bias, mlstm_norm_g, ret_norm_g, w_out_even, w_in_odd, pool_w, pool_scale, lambda_q1, lambda_k1, lambda_q2, lambda_k2, diff_norm_g, w_out_odd, router_w, router_b, moe_w1, moe_w3, moe_w2):
    bp, tp, d = x_prompt.shape
    bs, ts, _ = x_sample.shape
    past_len = cache_k.shape[2]
    n_layers = ada_w.shape[0]

    c_all = jnp.concatenate([c_prompt, c_sample, jnp.zeros((16 - bp - bs, d), F32)], axis=0)
    mod_all = _adaln(c_all, ada_w, ada_b)

    def mods_for(lo, stop):
        return [[mod_all[l, lo:stop, None, k * d:(k + 1) * d] for k in range(6)] for l in range(n_layers)]

    params = (norm_mix_g, norm_ffn_g, final_norm_g, w_in_even, mlstm_i_bias, mlstm_f_bias,
              mlstm_norm_g.reshape(-1, 512), ret_norm_g.reshape(-1, 512), w_out_even,
              w_in_odd, pool_w, pool_scale, lambda_q1, lambda_k1, lambda_q2, lambda_k2, diff_norm_g,
              w_out_odd, router_w, router_b)

    zeros = lambda *s: jnp.zeros(s, F32)
    p_states = (zeros(bp, N_HEADS, DH, DH), zeros(bp, N_HEADS, DH), zeros(bp, N_HEADS),
                zeros(bp, N_HEADS, DH, DH), zeros(bp, POOL_PAD - 1, 512), None, None)
    flat_w = lambda w: w.reshape((-1,) + w.shape[2:])
    moe_f32 = (flat_w(moe_w1), flat_w(moe_w3), flat_w(moe_w2))
    moe_bf16 = tuple(w.astype(BF16) for w in moe_f32)
    y_p, ev_p, od_p = _trunk(x_prompt, mods_for(0, bp), 0, tp, p_states, params, moe_bf16,
                             flat_moe=False, attn_tiles=(min(1024, tp), min(1024, tp)), proj_tm=min(512, tp),
                             moe_tm=min(1024, tp), hi=False)

    ts_pad = LANES
    x_s = jnp.concatenate([x_sample, zeros(bs, ts_pad - ts, d)], axis=1)
    s_states = (state_mlstm_c[0], state_mlstm_n[0], state_mlstm_m[0], state_ret[0], state_pool[0],
                cache_k[0], cache_v[0])
    y_s, ev_s, od_s = _trunk(x_s, mods_for(bp, bp + bs), past_len, ts, s_states, params,
                             moe_f32, flat_moe=True,
                             attn_tiles=(ts_pad, past_len + ts_pad), proj_tm=ts_pad, moe_tm=None, hi=True)
    y_s = y_s[:, :ts]

    return (y_p, y_s,
            ev_p[0][None], ev_p[1][None], ev_p[2][None], ev_p[3][None],
            od_p[0][None], od_p[1][None], od_p[2][None],
            ev_s[0][None], ev_s[1][None], ev_s[2][None], ev_s[3][None],
            od_s[0][None], od_s[1][None], od_s[2][None])
```

```python
import functools
import math

import jax
import jax.numpy as jnp
from jax import lax
from jax.experimental import pallas as pl
from jax.experimental.pallas import tpu as pltpu

F32 = jnp.float32
BF16 = jnp.bfloat16

EPS = 1e-6
ROPE_THETA = 10000.0
MASK_CHUNK = 64
MASK_SHIFT = 6
DH = 128
N_HEADS = 4
D_DIFF = 64
POOL_WINDOWS = (2, 4, 8, 16)
POOL_PAD = 16
N_EXPERTS = 16
EPG = 4
EVEN_CHUNKS_PER_STEP = 4
ATTN_KEY_SLAB = 256
MOE_ROW_BLOCK = 256
NEG_BIG = -1e30
LANES = 128
VT_ROWS = DH + 16
LOG2E = 1.4426950408889634
VMEM_LIMIT = 58 * 1024 * 1024

_NT = (((1,), (1,)), ((), ()))
_TN = (((0,), (0,)), ((), ()))


def _mm(a, b, hi, dims=None):
    if hi:
        a, b, prec = a.astype(F32), b.astype(F32), lax.Precision.HIGHEST
    else:
        a, b, prec = a.astype(BF16), b.astype(BF16), None
    if dims is None:
        return jnp.dot(a, b, precision=prec, preferred_element_type=F32)
    return lax.dot_general(a, b, dims, precision=prec, preferred_element_type=F32)


def _cdt(hi):
    return F32 if hi else BF16


def _rms_mod(x, g, sc, sh):
    ms = jnp.mean(x * x, axis=-1, keepdims=True)
    return (x * lax.rsqrt(ms + EPS) * g) * (1.0 + sc) + sh


def _params(sem):
    return pltpu.CompilerParams(dimension_semantics=sem, vmem_limit_bytes=VMEM_LIMIT)


def _resident(shape):
    zeros = (0,) * len(shape)
    return pl.BlockSpec(shape, lambda *args: zeros, pipeline_mode=pl.Buffered(1))


def _adaln_kernel(c_ref, w_ref, b_ref, o_ref):
    c = c_ref[...]
    o_ref[0] = _mm(c * jax.nn.sigmoid(c), w_ref[0], True) + b_ref[0]


def _adaln(c_all, ada_w, ada_b):
    n_layers, d, n6 = ada_w.shape
    rows = c_all.shape[0]
    bn = 1536
    return pl.pallas_call(
        _adaln_kernel,
        grid=(n_layers, n6 // bn),
        in_specs=[
            pl.BlockSpec((rows, d), lambda l, j: (0, 0)),
            pl.BlockSpec((1, d, bn), lambda l, j: (l, 0, j)),
            pl.BlockSpec((1, 1, bn), lambda l, j: (l, 0, j)),
        ],
        out_specs=pl.BlockSpec((1, rows, bn), lambda l, j: (l, 0, j)),
        out_shape=jax.ShapeDtypeStruct((n_layers, rows, n6), F32),
        compiler_params=_params(("parallel", "parallel")),
        name="adaln",
    )(c_all, ada_w, ada_b.reshape(n_layers, 1, n6))


def _log_sigmoid(x):
    return jnp.minimum(x, 0.0) - jnp.log1p(jnp.exp(-jnp.abs(x)))


def _even_kernel(x_ref, sc_ref, sh_ref, g1_ref, ng_ref, wrow_ref, wkt_ref, wgt_ref, gb_ref,
                 cosr_ref, sinr_ref, cost_ref, sint_ref, mng_ref, rng_ref, wout_ref,
                 c0_ref, m0_ref, s0_ref,
                 xo_ref, cf_ref, mf_ref, sf_ref,
                 r_scr, kt_scr, c_scr, m_scr, s_scr, y_scr, *, lc, n_c, lv, hi):
    t = pl.program_id(1)
    n_t = pl.num_programs(1)
    mm = functools.partial(_mm, hi=hi)

    @pl.when(t == 0)
    def _():
        c_scr[...] = c0_ref[0]
        m_scr[...] = m0_ref[0]
        s_scr[...] = s0_ref[0]

    x = x_ref[0]
    h = _rms_mod(x, ng_ref[...], sc_ref[0], sh_ref[0]).astype(_cdt(hi))
    r_scr[...] = mm(h, wrow_ref[...])
    kt_scr[...] = mm(wkt_ref[...], h, dims=_NT) * (DH ** -0.5)
    gt = mm(wgt_ref[...], h, dims=_NT) + gb_ref[...]

    lane = lax.broadcasted_iota(jnp.int32, (1, lc), 1)
    valid = lane < lv
    row_i = lax.broadcasted_iota(jnp.int32, (lc, lc), 0)
    col_i = lax.broadcasted_iota(jnp.int32, (lc, lc), 1)
    tri = col_i <= row_i
    eye = col_i == row_i
    upper = (row_i <= col_i).astype(F32)
    rel = (row_i - col_i).astype(F32)
    jrow = lane.astype(F32)
    jcol = lax.broadcasted_iota(jnp.int32, (lc, 1), 0).astype(F32)
    ones_col = (lax.broadcasted_iota(jnp.int32, (lc, DH), 1) == 0).astype(_cdt(hi))

    for c in range(n_c):
        rs = slice(c * lc, (c + 1) * lc)
        ig = gt[0:N_HEADS, rs]
        lf = jnp.where(valid, _log_sigmoid(gt[N_HEADS:2 * N_HEADS, rs]), 0.0)
        b_all = _mm(lf, upper, True)
        u_all = jnp.where(valid, ig - b_all, -jnp.inf)

        for hd in range(N_HEADS):
            sl = slice(hd * DH, (hd + 1) * DH)
            q = r_scr[rs, sl].astype(_cdt(hi))
            v = r_scr[rs, 512 + hd * DH:512 + (hd + 1) * DH].astype(_cdt(hi))
            vaug = jnp.concatenate([v, ones_col], axis=1)
            kt = kt_scr[sl, rs]
            u_row = u_all[hd:hd + 1]
            b_row = b_all[hd:hd + 1]
            m_prev = m_scr[hd]
            caug = c_scr[hd]

            u_mat = jnp.where(tri, u_row, -jnp.inf)
            r_col = jnp.maximum(m_prev, jnp.max(u_mat, axis=1, keepdims=True))
            b_col = jnp.sum(jnp.where(eye, b_row, 0.0), axis=1, keepdims=True)
            s_qk = mm(q, kt)
            p = jnp.exp(u_mat - r_col) * s_qk
            sc_col = jnp.exp(m_prev - r_col)
            nd = sc_col * mm(q, caug) + mm(p, vaug)
            num = nd[:, :DH]
            den = nd[:, DH:DH + 1]
            m_t = b_col + r_col
            h_m = num / jnp.maximum(jnp.abs(den), jnp.exp(-m_t))
            ms = jnp.mean(h_m * h_m, axis=-1, keepdims=True)
            h_m = h_m * lax.rsqrt(ms + EPS) * mng_ref[:, sl]
            gate = jax.nn.sigmoid(r_scr[rs, 1024 + hd * DH:1024 + (hd + 1) * DH])
            y_scr[rs, sl] = (gate * h_m).astype(y_scr.dtype)

            u_max = jnp.max(u_row, axis=1, keepdims=True)
            w_row = jnp.exp(u_row - u_max)
            kv = mm(kt * w_row, vaug)
            b_end = b_row[:, lc - 1:lc]
            a_max = b_end + u_max
            m_new = jnp.maximum(b_end + m_prev, a_max)
            c_scr[hd] = jnp.exp(b_end + m_prev - m_new) * caug + jnp.exp(a_max - m_new) * kv
            m_scr[hd] = m_new

        cos_r = cosr_ref[rs, :]
        sin_r = sinr_ref[rs, :]
        cos_t = cost_ref[:, rs]
        sin_t = sint_ref[:, rs]
        for hd in range(N_HEADS):
            lg = math.log(1.0 - 2.0 ** (-5.0 - hd))
            sl = slice(hd * DH, (hd + 1) * DH)
            rq = r_scr[rs, 1536 + hd * DH:1536 + (hd + 1) * DH]
            rq = (rq * cos_r + pltpu.roll(rq, DH // 2, 1) * sin_r).astype(_cdt(hi))
            rv = r_scr[rs, 2048 + hd * DH:2048 + (hd + 1) * DH].astype(_cdt(hi))
            kt = kt_scr[512 + hd * DH:512 + (hd + 1) * DH, rs]
            top, bot = kt[:DH // 2], kt[DH // 2:]
            kt = jnp.concatenate([top * cos_t - bot * sin_t, top * sin_t + bot * cos_t], axis=0)
            s_prev = s_scr[hd]

            decay = jnp.where(rel >= 0, jnp.exp(lg * jnp.maximum(rel, 0.0)), 0.0)
            scores = mm(rq, kt) * decay
            intra = mm(scores, rv)
            inter = mm(rq, s_prev) * jnp.exp(lg * (jcol + 1.0))
            o = intra + inter
            ms = jnp.mean(o * o, axis=-1, keepdims=True)
            o = o * lax.rsqrt(ms + EPS) * rng_ref[:, sl]
            g_in = r_scr[rs, 2560 + hd * DH:2560 + (hd + 1) * DH]
            y_scr[rs, 512 + hd * DH:512 + (hd + 1) * DH] = (g_in * jax.nn.sigmoid(g_in) * o).astype(y_scr.dtype)

            w_end = jnp.where(valid, jnp.exp(lg * (lv - 1.0 - jrow)), 0.0)
            s_scr[hd] = math.exp(lg * lv) * s_prev + mm(kt * w_end, rv)

    xo_ref[0] = x + g1_ref[0] * mm(y_scr[...], wout_ref[...])

    @pl.when(t == n_t - 1)
    def _():
        cf_ref[0] = c_scr[...]
        mf_ref[0] = m_scr[...]
        sf_ref[0] = s_scr[...]


def _even_mixer(x, sc, sh, g1, norm_g, w_in, i_bias, f_bias, mnorm_g, rnorm_g, w_out,
                c0, n0, m0, s0, pos0, t_valid, hi):
    bsz, t_pad, d = x.shape
    lc = LANES
    n_c = max(1, min(EVEN_CHUNKS_PER_STEP, t_pad // lc))
    tm = n_c * lc
    n_t = t_pad // tm
    lv = lc if t_pad > lc else t_valid
    assert t_pad == lc or t_valid == t_pad
    wdt = _cdt(hi)
    w = w_in
    cols = lambda a, b: w[:, a:b]
    w_row = jnp.concatenate([cols(0, 512), cols(1024, 1536), cols(1536, 2048),
                             cols(2056, 2568), cols(3080, 3592), cols(3592, 4104)], axis=1).astype(wdt)
    w_kt = jnp.concatenate([cols(512, 1024), cols(2568, 3080)], axis=1).T.astype(wdt)
    w_gt = jnp.concatenate([cols(2048, 2056).T, jnp.zeros((8, d), F32)], axis=0).astype(wdt)
    gb = jnp.concatenate([i_bias, f_bias, jnp.zeros((8,), F32)]).reshape(16, 1)

    half = DH // 2
    inv = jnp.power(ROPE_THETA, -jnp.arange(half, dtype=F32) * 2.0 / DH)
    pos = pos0 + jnp.arange(t_pad, dtype=jnp.int32)
    ang = pos.astype(F32)[:, None] * inv[None, :]
    cos, sin = jnp.cos(ang), jnp.sin(ang)
    cos_r = jnp.concatenate([cos, cos], axis=1)
    sin_r = jnp.concatenate([-sin, sin], axis=1)
    cos_t, sin_t = cos.T, sin.T

    caug0 = jnp.concatenate([c0, n0[..., None], jnp.zeros(c0.shape[:-1] + (DH - 1,), F32)], axis=-1)
    m0 = m0.reshape(bsz, N_HEADS, 1, 1)

    full = lambda shape: _resident(shape)
    per_b = lambda shape: pl.BlockSpec((1,) + shape, lambda b, t: (b,) + (0,) * len(shape))
    outs = pl.pallas_call(
        functools.partial(_even_kernel, lc=lc, n_c=n_c, lv=lv, hi=hi),
        grid=(bsz, n_t),
        in_specs=[
            pl.BlockSpec((1, tm, d), lambda b, t: (b, t, 0)),
            per_b((1, d)), per_b((1, d)), per_b((1, d)),
            full((1, d)),
            full((d, 3072)), full((1024, d)), full((16, d)), full((16, 1)),
            pl.BlockSpec((tm, DH), lambda b, t: (t, 0)),
            pl.BlockSpec((tm, DH), lambda b, t: (t, 0)),
            pl.BlockSpec((half, tm), lambda b, t: (0, t)),
            pl.BlockSpec((half, tm), lambda b, t: (0, t)),
            full((1, 512)), full((1, 512)), full((d, d)),
            per_b((N_HEADS, DH, 2 * DH)), per_b((N_HEADS, 1, 1)), per_b((N_HEADS, DH, DH)),
        ],
        out_specs=[
            pl.BlockSpec((1, tm, d), lambda b, t: (b, t, 0)),
            per_b((N_HEADS, DH, 2 * DH)), per_b((N_HEADS, 1, 1)), per_b((N_HEADS, DH, DH)),
        ],
        out_shape=[
            jax.ShapeDtypeStruct((bsz, t_pad, d), F32),
            jax.ShapeDtypeStruct((bsz, N_HEADS, DH, 2 * DH), F32),
            jax.ShapeDtypeStruct((bsz, N_HEADS, 1, 1), F32),
            jax.ShapeDtypeStruct((bsz, N_HEADS, DH, DH), F32),
        ],
        scratch_shapes=[
            pltpu.VMEM((tm, 3072), F32),
            pltpu.VMEM((1024, tm), F32),
            pltpu.VMEM((N_HEADS, DH, 2 * DH), F32),
            pltpu.VMEM((N_HEADS, 1, 1), F32),
            pltpu.VMEM((N_HEADS, DH, DH), F32),
            pltpu.VMEM((tm, d), wdt),
        ],
        compiler_params=_params(("parallel", "arbitrary")),
        name="even_mixer_hi" if hi else "even_mixer",
    )(x, sc, sh, g1, norm_g.reshape(1, d), w_row, w_kt, w_gt, gb,
      cos_r, sin_r, cos_t, sin_t, mnorm_g.reshape(1, 512), rnorm_g.reshape(1, 512),
      w_out.astype(wdt), caug0, m0, s0)
    x_new, caug, m_f, s_f = outs
    return x_new, caug[..., :DH], caug[..., DH], m_f.reshape(bsz, N_HEADS), s_f


def _odd_proj_kernel(x_ref, sc_ref, sh_ref, ng_ref, w_ref, wvt_ref, cos_ref, sin_ref,
                     pw_ref, ps_ref, past_ref,
                     krow_ref, vrow_ref, kb_ref, q0_ref, q1_ref, vt_ref, yp_ref, pool_ref,
                     r_scr, ext_scr, *, tm, tv, pos0, hi):
    t = pl.program_id(1)
    mm = functools.partial(_mm, hi=hi)

    @pl.when(t == 0)
    def _():
        ext_scr[0:POOL_PAD] = past_ref[0]

    x = x_ref[0]
    h = _rms_mod(x, ng_ref[...], sc_ref[0], sh_ref[0]).astype(_cdt(hi))
    r_scr[...] = mm(h, w_ref[...])
    vt = mm(wvt_ref[...], h, dims=_NT)
    for hd in range(N_HEADS):
        vt_ref[0, hd * VT_ROWS:hd * VT_ROWS + DH, :] = vt[hd * DH:(hd + 1) * DH].astype(vt_ref.dtype)
        vt_ref[0, hd * VT_ROWS + DH:(hd + 1) * VT_ROWS, :] = jnp.ones((VT_ROWS - DH, tm), vt_ref.dtype)
        vrow_ref[0, :, hd, :] = r_scr[:, 1536 + hd * DH:1536 + (hd + 1) * DH]

    ext_scr[POOL_PAD:POOL_PAD + tm] = r_scr[:, 0:512]
    pos = (pos0 + t * tm + lax.broadcasted_iota(jnp.int32, (tm, 1), 0)).astype(F32)
    for g, w in enumerate(POOL_WINDOWS):
        sl = slice(g * LANES, (g + 1) * LANES)
        pin = r_scr[:, sl]
        win = pin
        for j in range(1, w):
            win = win + ext_scr[POOL_PAD - j:POOL_PAD - j + tm, sl]
        cnt = jnp.minimum(pos + 1.0, float(w))
        dev = win / cnt - pin
        yp_ref[0, :, sl] = (mm(dev, pw_ref[g]) * ps_ref[:, sl]).astype(yp_ref.dtype)
    pool_ref[0] = ext_scr[tv:tv + POOL_PAD]
    ext_scr[0:POOL_PAD] = ext_scr[tm:tm + POOL_PAD]

    lane = lax.broadcasted_iota(jnp.int32, (1, LANES), 1)
    low_half = (lane & (D_DIFF - 1)) < (D_DIFF // 2)
    comp0 = lane < D_DIFF
    cos = cos_ref[...]
    sin = sin_ref[...]

    def rope(a):
        swapped = jnp.where(low_half, pltpu.roll(a, LANES - D_DIFF // 2, 1), pltpu.roll(a, D_DIFF // 2, 1))
        return a * cos + swapped * sin

    for hd in range(N_HEADS):
        sl = slice(hd * LANES, (hd + 1) * LANES)
        q = rope(r_scr[:, 512 + hd * LANES:512 + (hd + 1) * LANES]) * (D_DIFF ** -0.5 * LOG2E)
        q0_ref[0, :, sl] = jnp.where(comp0, q, 0.0).astype(q0_ref.dtype)
        q1_ref[0, :, sl] = jnp.where(comp0, 0.0, q).astype(q1_ref.dtype)
        k = rope(r_scr[:, 1024 + hd * LANES:1024 + (hd + 1) * LANES])
        krow_ref[0, :, hd, :] = k
        kb_ref[0, :, sl] = k.astype(kb_ref.dtype)


def _odd_proj(x, sc, sh, norm_g, w_in, pool_w, pool_scale, pool_past, pos0, t_valid, tm, hi):
    bsz, t_pad, d = x.shape
    n_t = t_pad // tm
    tv = tm if n_t > 1 else t_valid
    assert n_t == 1 or t_valid == t_pad
    wdt = _cdt(hi)
    w_all = w_in.astype(wdt)
    w_vt = w_in[:, 1536:2048].T.astype(wdt)

    half = D_DIFF // 2
    inv = jnp.power(ROPE_THETA, -jnp.arange(half, dtype=F32) * 2.0 / D_DIFF)
    pos = pos0 + jnp.arange(t_pad, dtype=jnp.int32)
    ang = pos.astype(F32)[:, None] * inv[None, :]
    cos, sin = jnp.cos(ang), jnp.sin(ang)
    cos_t = jnp.concatenate([cos, cos, cos, cos], axis=1)
    sin_t = jnp.concatenate([-sin, sin, -sin, sin], axis=1)
    past = jnp.concatenate([jnp.zeros((bsz, 1, 512), F32), pool_past], axis=1)

    full = lambda shape: _resident(shape)
    per_b = lambda shape: pl.BlockSpec((1,) + shape, lambda b, t: (b,) + (0,) * len(shape))
    tile = lambda n: pl.BlockSpec((1, tm, n), lambda b, t: (b, t, 0))
    head_tile = pl.BlockSpec((1, tm, N_HEADS, DH), lambda b, t: (b, t, 0, 0))
    f32_rows = jax.ShapeDtypeStruct((bsz, t_pad, N_HEADS, DH), F32)
    c_rows = jax.ShapeDtypeStruct((bsz, t_pad, 512), wdt)
    return pl.pallas_call(
        functools.partial(_odd_proj_kernel, tm=tm, tv=tv, pos0=pos0, hi=hi),
        grid=(bsz, n_t),
        in_specs=[
            tile(d), per_b((1, d)), per_b((1, d)), full((1, d)),
            full((d, 2048)), full((512, d)),
            pl.BlockSpec((tm, LANES), lambda b, t: (t, 0)),
            pl.BlockSpec((tm, LANES), lambda b, t: (t, 0)),
            full((4, LANES, LANES)), full((1, 512)), per_b((POOL_PAD, 512)),
        ],
        out_specs=[
            head_tile, head_tile, tile(512), tile(512), tile(512),
            pl.BlockSpec((1, N_HEADS * VT_ROWS, tm), lambda b, t: (b, 0, t)),
            tile(512), per_b((POOL_PAD, 512)),
        ],
        out_shape=[f32_rows, f32_rows, c_rows, c_rows, c_rows,
                   jax.ShapeDtypeStruct((bsz, N_HEADS * VT_ROWS, t_pad), wdt), c_rows,
                   jax.ShapeDtypeStruct((bsz, POOL_PAD, 512), F32)],
        scratch_shapes=[pltpu.VMEM((tm, 2048), F32), pltpu.VMEM((tm + POOL_PAD, 512), F32)],
        compiler_params=_params(("parallel", "arbitrary")),
        name="odd_proj_hi" if hi else "odd_proj",
    )(x, sc, sh, norm_g.reshape(1, d), w_all, w_vt, cos_t, sin_t,
      pool_w.astype(wdt), pool_scale.reshape(1, 512), past)


def _attn_kernel(qi_ref, kj_ref, last_ref,
                 q0_ref, q1_ref, kb_ref, vt_ref, x_ref, yp_ref, g1_ref, lamp_ref, dng_ref, wout_ref,
                 xo_ref, m_scr, acc_scr, *, tq, tk, q_pos0, tk_valid, lam_init, hi):
    s = pl.program_id(1)
    qi = qi_ref[s]
    kj = kj_ref[s]
    mm = functools.partial(_mm, hi=hi)

    @pl.when(kj == 0)
    def _():
        m_scr[...] = jnp.full_like(m_scr, NEG_BIG)
        acc_scr[...] = jnp.zeros_like(acc_scr)

    q_refs = (q0_ref, q1_ref)

    def scores(i):
        sl = slice((i // 2) * LANES, (i // 2 + 1) * LANES)
        return mm(kb_ref[0, :, sl], q_refs[i % 2][0, :, sl], dims=_NT)

    def step(masked):
        if masked:
            kpos = kj * tk + lax.broadcasted_iota(jnp.int32, (tk, tq), 0)
            qpos = q_pos0 + qi * tq + lax.broadcasted_iota(jnp.int32, (tk, tq), 1)
            visible = ((kpos >> MASK_SHIFT) <= (qpos >> MASK_SHIFT)) & (kpos < tk_valid)
        st_next = scores(0)
        for i in range(2 * N_HEADS):
            st = st_next
            if i + 1 < 2 * N_HEADS:
                st_next = scores(i + 1)
            if masked:
                st = jnp.where(visible, st, NEG_BIG)
            hd = i // 2
            vt = vt_ref[0, hd * VT_ROWS:(hd + 1) * VT_ROWS, :]
            m_prev = m_scr[i:i + 1]
            m_new = jnp.maximum(m_prev, jnp.max(st, axis=0, keepdims=True))
            alpha = jnp.exp2(m_prev - m_new)
            pv = None
            for k0 in range(0, tk, ATTN_KEY_SLAB):
                p = jnp.exp2(st[k0:k0 + ATTN_KEY_SLAB] - m_new)
                part = mm(vt[:, k0:k0 + ATTN_KEY_SLAB], p)
                pv = part if pv is None else pv + part
            acc_scr[i] = alpha * acc_scr[i] + pv
            m_scr[i:i + 1] = m_new

    fully_visible = ((kj + 1) * tk - 1) // MASK_CHUNK <= (q_pos0 + qi * tq) // MASK_CHUNK
    no_pad = (kj + 1) * tk <= tk_valid
    clean = jnp.logical_and(fully_visible, no_pad)
    pl.when(clean)(lambda: step(False))
    pl.when(jnp.logical_not(clean))(lambda: step(True))

    @pl.when(last_ref[s] == 1)
    def _():
        lp = lamp_ref[...]
        lam = (jnp.exp(jnp.sum(lp[0:1] * lp[1:2], axis=1, keepdims=True))
               - jnp.exp(jnp.sum(lp[2:3] * lp[3:4], axis=1, keepdims=True)) + lam_init)
        y = mm(yp_ref[0], wout_ref[0:512, :])
        for hd in range(N_HEADS):
            o0 = acc_scr[2 * hd, 0:DH] / acc_scr[2 * hd, DH:DH + 1]
            o1 = acc_scr[2 * hd + 1, 0:DH] / acc_scr[2 * hd + 1, DH:DH + 1]
            ot = o0 - lam * o1
            ms = jnp.mean(ot * ot, axis=0, keepdims=True)
            ot = (ot * lax.rsqrt(ms + EPS) * dng_ref[...]) * (1.0 - lam_init)
            y = y + mm(ot, wout_ref[512 + hd * LANES:512 + (hd + 1) * LANES, :], dims=_TN)
        xo_ref[0] = x_ref[0] + g1_ref[0] * y


def _attention(q0, q1, kb, vt, x, yp, g1, lam_params, dnorm_g, w_out, q_pos0, tk_valid, lam_init, tq, tk, hi):
    bsz, t_q, d = x.shape
    t_k = kb.shape[1]
    n_q, n_k = t_q // tq, t_k // tk
    qi, kj, last = [], [], []
    for i in range(n_q):
        q_chunk_last = (q_pos0 + (i + 1) * tq - 1) // MASK_CHUNK
        k_last = min((q_chunk_last + 1) * MASK_CHUNK, tk_valid) - 1
        j_last = min(k_last // tk, n_k - 1)
        for j in range(j_last + 1):
            qi.append(i)
            kj.append(j)
            last.append(1 if j == j_last else 0)
    qi, kj, last = (jnp.asarray(a, jnp.int32) for a in (qi, kj, last))
    n_steps = int(qi.shape[0])

    q_spec = pl.BlockSpec((1, tq, 512), lambda b, s, qi, kj, la: (b, qi[s], 0))
    full = lambda shape: _resident(shape)
    grid_spec = pltpu.PrefetchScalarGridSpec(
        num_scalar_prefetch=3,
        grid=(bsz, n_steps),
        in_specs=[
            q_spec, q_spec,
            pl.BlockSpec((1, tk, 512), lambda b, s, qi, kj, la: (b, kj[s], 0)),
            pl.BlockSpec((1, N_HEADS * VT_ROWS, tk), lambda b, s, qi, kj, la: (b, 0, kj[s])),
            pl.BlockSpec((1, tq, d), lambda b, s, qi, kj, la: (b, qi[s], 0)),
            q_spec,
            pl.BlockSpec((1, 1, d), lambda b, s, qi, kj, la: (b, 0, 0)),
            full((4, D_DIFF)), full((LANES, 1)), full((d, d)),
        ],
        out_specs=pl.BlockSpec((1, tq, d), lambda b, s, qi, kj, la: (b, qi[s], 0)),
        scratch_shapes=[
            pltpu.VMEM((2 * N_HEADS, tq), F32),
            pltpu.VMEM((2 * N_HEADS, VT_ROWS, tq), F32),
        ],
    )
    return pl.pallas_call(
        functools.partial(_attn_kernel, tq=tq, tk=tk, q_pos0=q_pos0, tk_valid=tk_valid,
                          lam_init=lam_init, hi=hi),
        grid_spec=grid_spec,
        out_shape=jax.ShapeDtypeStruct((bsz, t_q, d), F32),
        compiler_params=_params(("parallel", "arbitrary")),
        name="diff_attention_hi" if hi else "diff_attention",
    )(qi, kj, last, q0, q1, kb, vt, x, yp, g1, lam_params, dnorm_g.reshape(LANES, 1),
      w_out.astype(_cdt(hi)))


def _route(h, rw, rb, hi):
    if hi:
        logits = _mm(h, rw, True)
    else:
        h_hi = h.astype(BF16)
        h_lo = (h - h_hi.astype(F32)).astype(BF16)
        both = jnp.dot(h_hi, rw, preferred_element_type=F32)
        logits = (both[:, :LANES] + both[:, LANES:]) + jnp.dot(h_lo, rw[:, :LANES], preferred_element_type=F32)
    s = jax.nn.sigmoid(logits)
    sel = s + rb
    lane_i = lax.broadcasted_iota(jnp.int32, sel.shape, 1)
    lane = lane_i.astype(F32)
    group = (lane_i >> 2).astype(F32)

    def top2(vals):
        m1 = jnp.max(vals, axis=1, keepdims=True)
        i1 = jnp.min(jnp.where(vals == m1, lane, float(N_EXPERTS)), axis=1, keepdims=True)
        rest = jnp.where(lane == i1, -jnp.inf, vals)
        m2 = jnp.max(rest, axis=1, keepdims=True)
        i2 = jnp.min(jnp.where(rest == m2, lane, float(N_EXPERTS)), axis=1, keepdims=True)
        return m1, i1, m2, i2

    best = None
    for g in range(N_EXPERTS // EPG):
        m1, _, m2, _ = top2(jnp.where(group == float(g), sel, -jnp.inf))
        score = m1 + m2
        if best is None:
            best, gbest = score, jnp.zeros_like(score)
        else:
            better = score > best
            best = jnp.where(better, score, best)
            gbest = jnp.where(better, float(g), gbest)
    _, i1, _, i2 = top2(jnp.where(group == gbest, sel, -jnp.inf))
    w1 = jnp.sum(jnp.where(lane == i1, s, 0.0), axis=1, keepdims=True)
    w2 = jnp.sum(jnp.where(lane == i2, s, 0.0), axis=1, keepdims=True)
    tot = w1 + w2
    return jnp.where(lane == i1, w1 / tot, 0.0) + jnp.where(lane == i2, w2 / tot, 0.0), gbest


def _moe_kernel(x_ref, sc_ref, sh_ref, g2_ref, ng_ref, rw_ref, rb_ref, w1_ref, w3_ref, w2_ref, fg_ref,
                o_ref, haug_scr, key_scr, keyt_scr, xg_scr, gg_scr, yg_scr, acc_scr, cnt_smem,
                *, final_norm, hi, rb_rows):
    EXPERTS_PER_STEP = w1_ref.shape[0]
    STEPS_PER_GROUP = EPG // EXPERTS_PER_STEP
    step = pl.program_id(1)
    grp = step // STEPS_PER_GROUP
    tm, d = x_ref.shape[1], x_ref.shape[2]
    cdt = _cdt(hi)
    mm = functools.partial(_mm, hi=hi)
    lane = lax.broadcasted_iota(jnp.int32, (1, LANES), 1)

    @pl.when(step == 0)
    def _():
        h = _rms_mod(x_ref[0], ng_ref[...], sc_ref[0], sh_ref[0])
        gates, gbest = _route(h, rw_ref[...], rb_ref[...], hi)
        g_hi = gates.astype(BF16).astype(F32)
        rem = gates - g_hi
        g_mid = rem.astype(BF16).astype(F32)
        g_lo = (rem - g_mid).astype(BF16).astype(F32)
        pieces = g_hi + pltpu.roll(g_mid, N_EXPERTS, 1) + pltpu.roll(g_lo, 2 * N_EXPERTS, 1)
        haug_scr[:, 0:d] = h.astype(cdt)
        haug_scr[:, d:d + LANES] = pieces.astype(cdt)
        member = (gbest == lane.astype(F32)).astype(BF16)
        rows_i = lax.broadcasted_iota(jnp.int32, (tm, tm), 0)
        cols_i = lax.broadcasted_iota(jnp.int32, (tm, tm), 1)
        before = (cols_i < rows_i).astype(BF16)
        rank = jnp.dot(before, member, preferred_element_type=F32)
        key = jnp.where(member > 0, rank, -1.0)
        key_scr[...] = key
        eye = (lax.broadcasted_iota(jnp.int32, (8, LANES), 0)
               == lax.broadcasted_iota(jnp.int32, (8, LANES), 1)).astype(F32)
        keyt_scr[...] = _mm(eye, key, True, dims=_NT)
        for gi in range(N_EXPERTS // EPG):
            cnt_smem[gi] = jnp.sum(jnp.where(lane == gi, member.astype(F32), 0.0)).astype(jnp.int32)
        acc_scr[...] = jnp.zeros_like(acc_scr)

    n_blk = (cnt_smem[grp] + rb_rows - 1) // rb_rows

    @pl.when(step % STEPS_PER_GROUP == 0)
    def _():
        key_row = keyt_scr[pl.ds(grp, 1), :]

        def gather(blk, carry):
            off = pl.multiple_of(blk * rb_rows, rb_rows)
            row_id = (off + lax.broadcasted_iota(jnp.int32, (rb_rows, 1), 0)).astype(F32)
            onehot = (key_row == row_id).astype(cdt)
            xa = mm(onehot, haug_scr[...])
            xg_scr[pl.ds(off, rb_rows), :] = xa[:, 0:d].astype(cdt)
            pc = xa[:, d:d + LANES]
            gg_scr[pl.ds(off, rb_rows), :] = (pc + pltpu.roll(pc, LANES - N_EXPERTS, 1)
                                              + pltpu.roll(pc, LANES - 2 * N_EXPERTS, 1))
            yg_scr[pl.ds(off, rb_rows), :] = jnp.zeros((rb_rows, d), F32)
            return carry

        lax.fori_loop(0, n_blk, gather, 0)

    def expert(blk, carry):
        off = pl.multiple_of(blk * rb_rows, rb_rows)
        xb = xg_scr[pl.ds(off, rb_rows), :]
        gates = gg_scr[pl.ds(off, rb_rows), :]
        y = yg_scr[pl.ds(off, rb_rows), :]
        for j in range(EXPERTS_PER_STEP):
            a = mm(xb, w1_ref[j])
            b = mm(xb, w3_ref[j])
            ff = mm(a * jax.nn.sigmoid(a) * b, w2_ref[j])
            e = step * EXPERTS_PER_STEP + j
            y = y + jnp.sum(jnp.where(lane == e, gates, 0.0), axis=1, keepdims=True) * ff
        yg_scr[pl.ds(off, rb_rows), :] = y
        return carry

    lax.fori_loop(0, n_blk, expert, 0)

    @pl.when(step % STEPS_PER_GROUP == STEPS_PER_GROUP - 1)
    def _():
        key_col = jnp.sum(jnp.where(lane == grp, key_scr[...], 0.0), axis=1, keepdims=True)

        def scatter(blk, carry):
            off = pl.multiple_of(blk * rb_rows, rb_rows)
            col_id = (off + lax.broadcasted_iota(jnp.int32, (1, rb_rows), 1)).astype(F32)
            onehot_t = (key_col == col_id).astype(cdt)
            acc_scr[...] += mm(onehot_t, yg_scr[pl.ds(off, rb_rows), :])
            return carry

        lax.fori_loop(0, n_blk, scatter, 0)

    @pl.when(step == N_EXPERTS // EXPERTS_PER_STEP - 1)
    def _():
        y = x_ref[0] + g2_ref[0] * acc_scr[...]
        if final_norm:
            ms = jnp.mean(y * y, axis=-1, keepdims=True)
            y = y * lax.rsqrt(ms + EPS) * fg_ref[...]
        o_ref[0] = y


def _moe(x, sc, sh, g2, norm_g, router_w, router_b, w1, w3, w2, layer, final_g, final_norm, tm, hi):
    bsz, t_len, d = x.shape
    n_t = t_len // tm
    if hi:
        rb_rows = min(tm, -(-(tm * 5 // 16) // 16) * 16)
    else:
        rb_rows = min(tm, MOE_ROW_BLOCK)
    cap_rows = -(-tm // rb_rows) * rb_rows
    rows = sc.shape[1]
    EXPERTS_PER_STEP = EPG // 2 if hi else EPG
    lw = layer * (N_EXPERTS // EXPERTS_PER_STEP)
    rw_pad = jnp.concatenate([router_w, jnp.zeros((d, LANES - N_EXPERTS), F32)], axis=1)
    if not hi:
        rw_hi = rw_pad.astype(BF16)
        rw_pad = jnp.concatenate([rw_hi, (rw_pad - rw_hi.astype(F32)).astype(BF16)], axis=1)
    rb_pad = jnp.concatenate([router_b, jnp.zeros((LANES - N_EXPERTS,), F32)]).reshape(1, LANES)
    if rows == 1:
        mod_spec = pl.BlockSpec((1, 1, d), lambda i, e: (i // n_t, 0, 0))
    else:
        mod_spec = pl.BlockSpec((1, tm, d), lambda i, e: (i // n_t, i % n_t, 0))
    x_spec = pl.BlockSpec((1, tm, d), lambda i, e: (i // n_t, i % n_t, 0))
    full = lambda shape: _resident(shape)
    dff = w1.shape[-1]
    return pl.pallas_call(
        functools.partial(_moe_kernel, final_norm=final_norm, hi=hi, rb_rows=rb_rows),
        grid=(bsz * n_t, N_EXPERTS // EXPERTS_PER_STEP),
        in_specs=[
            x_spec, mod_spec, mod_spec, mod_spec, full((1, d)),
            full(rw_pad.shape), full((1, LANES)),
            pl.BlockSpec((EXPERTS_PER_STEP, d, dff), lambda i, e: (lw + e, 0, 0)),
            pl.BlockSpec((EXPERTS_PER_STEP, d, dff), lambda i, e: (lw + e, 0, 0)),
            pl.BlockSpec((EXPERTS_PER_STEP, dff, d), lambda i, e: (lw + e, 0, 0)),
            full((1, d)),
        ],
        out_specs=x_spec,
        out_shape=jax.ShapeDtypeStruct((bsz, t_len, d), F32),
        scratch_shapes=[
            pltpu.VMEM((tm, d + LANES), _cdt(hi)),
            pltpu.VMEM((tm, LANES), F32),
            pltpu.VMEM((8, tm), F32),
            pltpu.VMEM((cap_rows, d), _cdt(hi)),
            pltpu.VMEM((cap_rows, LANES), F32),
            pltpu.VMEM((cap_rows, d), F32),
            pltpu.VMEM((tm, d), F32),
            pltpu.SMEM((N_EXPERTS // EPG,), jnp.int32),
        ],
        compiler_params=_params(("parallel", "arbitrary")),
        name="moe_hi" if hi else "moe",
    )(x, sc, sh, g2, norm_g.reshape(1, d), rw_pad, rb_pad, w1, w3, w2, final_g.reshape(1, d))


def _trunk(x, mods, pos0, t_valid, states, params, moe_weights, flat_moe, attn_tiles, proj_tm, moe_tm, hi):
    (norm_mix_g, norm_ffn_g, final_norm_g, w_in_even, i_bias, f_bias, mnorm_g, rnorm_g, w_out_even,
     w_in_odd, pool_w, pool_scale, lq1, lk1, lq2, lk2, dnorm_g, w_out_odd, router_w, router_b) = params
    c0, n0, m0, s0, pool_past, k_past, v_past = states
    bsz, t_pad, d = x.shape
    w1, w3, w2 = moe_weights

    def run_moe(x, layer, final):
        sc2, sh2, g2 = mods[layer][4], mods[layer][3], mods[layer][5]
        if flat_moe:
            n_rows = bsz * t_valid
            flat = lambda a: jnp.broadcast_to(a, (bsz, t_valid, d)).reshape(1, n_rows, d)
            y = _moe(x[:, :t_valid].reshape(1, n_rows, d), flat(sc2), flat(sh2), flat(g2), norm_ffn_g[layer],
                     router_w, router_b, w1, w3, w2, layer, final_norm_g, final, n_rows, hi)
            y = y.reshape(bsz, t_valid, d)
            return jnp.concatenate([y, jnp.zeros((bsz, t_pad - t_valid, d), F32)], axis=1)
        return _moe(x, sc2, sh2, g2, norm_ffn_g[layer], router_w, router_b,
                    w1, w3, w2, layer, final_norm_g, final, moe_tm, hi)

    sh1, sc1, g1 = mods[0][0], mods[0][1], mods[0][2]
    x, c_f, n_f, m_f, s_f = _even_mixer(x, sc1, sh1, g1, norm_mix_g[0], w_in_even[0], i_bias[0], f_bias[0],
                                        mnorm_g[0], rnorm_g[0], w_out_even[0], c0, n0, m0, s0, pos0, t_valid, hi)
    x = run_moe(x, 0, False)

    layer = 1
    lam_init = 0.8 - 0.6 * math.exp(-0.3 * layer)
    sh1, sc1, g1 = mods[1][0], mods[1][1], mods[1][2]
    k_rows, v_rows, kb, q0, q1, vt, yp, pool_new = _odd_proj(
        x, sc1, sh1, norm_mix_g[1], w_in_odd[0], pool_w[0], pool_scale[0], pool_past, pos0, t_valid, proj_tm, hi)
    if k_past is None:
        tk_valid = t_valid
    else:
        past_len = k_past.shape[1]
        tk_valid = past_len + t_valid
        kb = jnp.concatenate([k_past.reshape(bsz, past_len, 512).astype(kb.dtype), kb], axis=1)
        vt_past = jnp.concatenate([jnp.transpose(v_past, (0, 2, 3, 1)),
                                   jnp.ones((bsz, N_HEADS, VT_ROWS - DH, past_len), F32)], axis=2)
        vt = jnp.concatenate([vt_past.reshape(bsz, N_HEADS * VT_ROWS, past_len).astype(vt.dtype), vt], axis=2)
    lam_params = jnp.stack([lq1[0], lk1[0], lq2[0], lk2[0]])
    tq, tk = attn_tiles
    x = _attention(q0, q1, kb, vt, x, yp, g1, lam_params, dnorm_g[0], w_out_odd[0],
                   pos0, tk_valid, lam_init, tq, tk, hi)
    x = run_moe(x, 1, True)
    return x, (c_f, n_f, m_f, s_f), (pool_new[:, 1:], k_rows[:, :t_valid], v_rows[:, :t_valid])


def kernel(x_prompt, x_sample, c_prompt, c_sample, state_mlstm_c, state_mlstm_n, state_mlstm_m, state_ret, state_pool, cache_k, cache_v, ada_w, ada_b, norm_mix_g, norm_ffn_g, final_norm_g, w_in_even, mlstm_i_bias, mlstm_f_bias, mlstm_norm_g, ret_norm_g, w_out_even, w_in_odd, pool_w, pool_scale, lambda_q1, lambda_k1, lambda_q2, lambda_k2, diff_norm_g, w_out_odd, router_w, router_b, moe_w1, moe_w3, moe_w2):
    bp, tp, d = x_prompt.shape
    bs, ts, _ = x_sample.shape
    past_len = cache_k.shape[2]
    n_layers = ada_w.shape[0]

    c_all = jnp.concatenate([c_prompt, c_sample, jnp.zeros((16 - bp - bs, d), F32)], axis=0)
    mod_all = _adaln(c_all, ada_w, ada_b)

    def mods_for(lo, stop):
        return [[mod_all[l, lo:stop, None, k * d:(k + 1) * d] for k in range(6)] for l in range(n_layers)]

    params = (norm_mix_g, norm_ffn_g, final_norm_g, w_in_even, mlstm_i_bias, mlstm_f_bias,
              mlstm_norm_g.reshape(-1, 512), ret_norm_g.reshape(-1, 512), w_out_even,
              w_in_odd, pool_w, pool_scale, lambda_q1, lambda_k1, lambda_q2, lambda_k2, diff_norm_g,
              w_out_odd, router_w, router_b)

    zeros = lambda *s: jnp.zeros(s, F32)
    p_states = (zeros(bp, N_HEADS, DH, DH), zeros(bp, N_HEADS, DH), zeros(bp, N_HEADS),
                zeros(bp, N_HEADS, DH, DH), zeros(bp, POOL_PAD - 1, 512), None, None)
    flat_w = lambda w: w.reshape((-1,) + w.shape[2:])
    moe_f32 = (flat_w(moe_w1), flat_w(moe_w3), flat_w(moe_w2))
    moe_bf16 = tuple(w.astype(BF16) for w in moe_f32)
    y_p, ev_p, od_p = _trunk(x_prompt, mods_for(0, bp), 0, tp, p_states, params, moe_bf16,
                             flat_moe=False, attn_tiles=(min(1024, tp), min(1024, tp)), proj_tm=min(512, tp),
                             moe_tm=min(1024, tp), hi=False)

    ts_pad = LANES
    x_s = jnp.concatenate([x_sample, zeros(bs, ts_pad - ts, d)], axis=1)
    s_states = (state_mlstm_c[0], state_mlstm_n[0], state_mlstm_m[0], state_ret[0], state_pool[0],
                cache_k[0], cache_v[0])
    y_s, ev_s, od_s = _trunk(x_s, mods_for(bp, bp + bs), past_len, ts, s_states, params,
                             moe_f32, flat_moe=True,
                             attn_tiles=(ts_pad, past_len + ts_pad), proj_tm=ts_pad, moe_tm=None, hi=True)
    y_s = y_s[:, :ts]

    return (y_p, y_s,
            ev_p[0][None], ev_p[1][None], ev_p[2][None], ev_p[3][None],
            od_p[0][None], od_p[1][None], od_p[2][None],
            ev_s[0][None], ev_s[1][None], ev_s[2][None], ev_s[3][None],
            od_s[0][None], od_s[1][None], od_s[2][None])
```
